```python
import math
import jax, jax.numpy as jnp
from jax import lax
import numpy as np

D_MODEL = 1024
BATCH = 4
SEQ = 8192
DEPTH = 2

N_A_LAYERS = DEPTH // 2
N_B_LAYERS = DEPTH - N_A_LAYERS
EPS = 1e-6

D_FF = 2816
FFN_RES = 0.5

HGRN_HEADS = 8
HGRN_EXPAND = 128
HGRN_F_DIM = HGRN_HEADS * HGRN_EXPAND
HGRN_V_HEAD = D_MODEL // HGRN_HEADS
HGRN_CHUNK = 64
HGRN_IN_DIM = 2 * HGRN_F_DIM + 2 * D_MODEL

DIFF_HEADS = 8
DIFF_HEAD_DIM = D_MODEL // (2 * DIFF_HEADS)
DIFF_V_DIM = 2 * DIFF_HEAD_DIM
DIFF_Q_DIM = 2 * DIFF_HEADS * DIFF_HEAD_DIM
DIFF_KV_DIM = 2 * DIFF_HEADS * DIFF_HEAD_DIM + DIFF_HEADS * DIFF_V_DIM
Q_BLOCK = 128

N_BUCKETS = 32
MAX_DISTANCE = 128

kernel_name = "yoco_hgrn2_diffattn_macaron"


def rmsnorm(x, w):
    xf = x.astype(jnp.float32)
    y = xf * lax.rsqrt(jnp.mean(xf * xf, axis=-1, keepdims=True) + EPS)
    return (y * w.astype(jnp.float32)).astype(x.dtype)


def swiglu(x, w_in, w_out):
    g, u = jnp.split(x @ w_in, 2, axis=-1)
    return (jax.nn.silu(g) * u) @ w_out


def to_heads(t, n_heads):
    b, s, _ = t.shape
    return t.reshape(b, s, n_heads, -1).transpose(0, 2, 1, 3)


def chunk_gated_recurrence(q, k, v, logf):
    b_, h_, s_, dk = q.shape
    dv = v.shape[-1]
    c = HGRN_CHUNK
    n = s_ // c
    q = q.reshape(b_, h_, n, c, dk)
    k = k.reshape(b_, h_, n, c, dk)
    v = v.reshape(b_, h_, n, c, dv)
    bcum = jnp.cumsum(logf.reshape(b_, h_, n, c, dk), axis=3)
    b_last = bcum[:, :, :, -1:, :]
    b_mid = bcum[:, :, :, c // 2 - 1:c // 2, :]
    u = jnp.einsum('bhncd,bhnce->bhnde', k * jnp.exp(b_last - bcum), v)
    decay = jnp.exp(b_last[:, :, :, 0, :])

    def step(s, xs):
        dec, un = xs
        return dec[..., None] * s + un, s

    s0 = jnp.zeros((b_, h_, dk, dv), jnp.float32)
    _, s_prev = lax.scan(step, s0, (jnp.moveaxis(decay, 2, 0), jnp.moveaxis(u, 2, 0)))
    s_prev = jnp.moveaxis(s_prev, 0, 2)
    o_inter = jnp.einsum('bhncd,bhnde->bhnce', q * jnp.exp(bcum), s_prev)
    qi = q * jnp.exp(bcum - b_mid)
    ki = k * jnp.exp(b_mid - bcum)
    a = jnp.einsum('bhncd,bhnmd->bhncm', qi, ki)
    mask = jnp.tril(jnp.ones((c, c), dtype=bool))
    a = jnp.where(mask, a, 0.0)
    o_intra = jnp.einsum('bhncm,bhnme->bhnce', a, v)
    return (o_inter + o_intra).reshape(b_, h_, s_, dv)


def hgrn2_mixer(h, w_in, lb, gnorm_w, w_out):
    b_, s_, _ = h.shape
    q, f, i, g = jnp.split(h @ w_in, [HGRN_F_DIM, 2 * HGRN_F_DIM, 2 * HGRN_F_DIM + D_MODEL], axis=-1)
    q = jax.nn.silu(q.astype(jnp.float32))
    f = lb + (1.0 - lb) * jax.nn.sigmoid(f.astype(jnp.float32))
    k = 1.0 - f
    o = chunk_gated_recurrence(to_heads(q, HGRN_HEADS), to_heads(k, HGRN_HEADS),
                               to_heads(i.astype(jnp.float32), HGRN_HEADS),
                               to_heads(jnp.log(f), HGRN_HEADS))
    o = o.transpose(0, 2, 1, 3)
    gate = jax.nn.silu(g.astype(jnp.float32)).reshape(b_, s_, HGRN_HEADS, HGRN_V_HEAD)
    o = rmsnorm(o, gnorm_w) * gate
    return (o.reshape(b_, s_, D_MODEL).astype(h.dtype) @ w_out).astype(h.dtype)


def t5_bucket(rel):
    n = jnp.maximum(rel, 0)
    max_exact = N_BUCKETS // 2
    nf = jnp.maximum(n, 1).astype(jnp.float32)
    large = max_exact + (jnp.log(nf / max_exact) / math.log(MAX_DISTANCE / max_exact)
                         * (N_BUCKETS - max_exact)).astype(jnp.int32)
    large = jnp.minimum(large, N_BUCKETS - 1)
    return jnp.where(n < max_exact, n, large)


def shared_kv(h, kv_norm_w, w_kv):
    kv = rmsnorm(h, kv_norm_w) @ w_kv
    hd = DIFF_HEADS * DIFF_HEAD_DIM
    k1, k2, v = jnp.split(kv, [hd, 2 * hd], axis=-1)
    return (to_heads(k1, DIFF_HEADS), to_heads(k2, DIFF_HEADS), to_heads(v, DIFF_HEADS))


def diff_attention(h, k1, k2, v, w_q, lam_p, subln_w, w_out, rel_bias, lambda_init):
    b_, s_, _ = h.shape
    scale = DIFF_HEAD_DIM ** -0.5
    q1, q2 = jnp.split(h @ w_q, 2, axis=-1)
    nb = s_ // Q_BLOCK

    def blocks(t):
        t = (to_heads(t, DIFF_HEADS) * scale).astype(h.dtype)
        return t.reshape(b_, DIFF_HEADS, nb, Q_BLOCK, DIFF_HEAD_DIM).transpose(2, 0, 1, 3, 4)

    lp = lam_p.astype(jnp.float32)
    lam = jnp.exp(jnp.sum(lp[0] * lp[1])) - jnp.exp(jnp.sum(lp[2] * lp[3])) + lambda_init
    k_pos = jnp.arange(s_, dtype=jnp.int32)
    table = rel_bias.astype(jnp.float32)

    def one_block(args):
        q1c, q2c, blk = args
        q_pos = blk * Q_BLOCK + jnp.arange(Q_BLOCK, dtype=jnp.int32)
        rel = q_pos[:, None] - k_pos[None, :]
        bias = table[t5_bucket(rel)].transpose(2, 0, 1)[None]
        causal = rel >= 0

        def probs(qc, kk):
            sc = jnp.einsum('bhqd,bhkd->bhqk', qc, kk).astype(jnp.float32) + bias
            return jax.nn.softmax(jnp.where(causal, sc, -jnp.inf), axis=-1)

        p = probs(q1c, k1) - lam * probs(q2c, k2)
        return jnp.einsum('bhqk,bhke->bhqe', p.astype(v.dtype), v)

    o = lax.map(one_block, (blocks(q1), blocks(q2), jnp.arange(nb, dtype=jnp.int32)))
    o = o.transpose(1, 0, 3, 2, 4).reshape(b_, s_, DIFF_HEADS, DIFF_V_DIM)
    o = rmsnorm(o, subln_w) * (1.0 - lambda_init)
    return (o.reshape(b_, s_, D_MODEL) @ w_out).astype(h.dtype)


def setup_inputs(seed: int = 0) -> dict:
    key = jax.random.key(seed)
    ks = jax.random.split(key, 16)
    f32 = jnp.float32

    def nrm(k, shape, fan_in):
        return jax.random.normal(k, shape, f32) * fan_in ** -0.5

    def gain(k, shape):
        return 1.0 + 0.02 * jax.random.normal(k, shape, f32)

    return {
        "x": jax.random.normal(ks[0], (BATCH, SEQ, D_MODEL), f32),
        "norm_w": gain(ks[1], (DEPTH, 3, D_MODEL)),
        "ffn_w_in": nrm(ks[2], (DEPTH, 2, D_MODEL, 2 * D_FF), D_MODEL),
        "ffn_w_out": nrm(ks[3], (DEPTH, 2, D_FF, D_MODEL), D_FF),
        "hgrn_w_in": nrm(ks[4], (N_A_LAYERS, D_MODEL, HGRN_IN_DIM), D_MODEL),
        "hgrn_lower_bounds": 0.1 * jax.random.normal(ks[5], (DEPTH, HGRN_F_DIM), f32),
        "hgrn_gnorm_w": gain(ks[6], (N_A_LAYERS, HGRN_V_HEAD)),
        "hgrn_w_out": nrm(ks[7], (N_A_LAYERS, D_MODEL, D_MODEL), D_MODEL),
        "kv_norm_w": gain(ks[8], (D_MODEL,)),
        "w_kv": nrm(ks[9], (D_MODEL, DIFF_KV_DIM), D_MODEL),
        "rel_bias": 0.5 * jax.random.normal(ks[10], (N_BUCKETS, DIFF_HEADS), f32),
        "diff_w_q": nrm(ks[11], (N_B_LAYERS, D_MODEL, DIFF_Q_DIM), D_MODEL),
        "diff_lambda": 0.1 * jax.random.normal(ks[12], (N_B_LAYERS, 4, DIFF_HEAD_DIM), f32),
        "diff_subln_w": gain(ks[13], (N_B_LAYERS, DIFF_V_DIM)),
        "diff_w_out": nrm(ks[14], (N_B_LAYERS, D_MODEL, D_MODEL), D_MODEL),
        "final_norm_w": gain(ks[15], (D_MODEL,)),
    }


def reference(x, norm_w, ffn_w_in, ffn_w_out, hgrn_w_in, hgrn_lower_bounds, hgrn_gnorm_w,
              hgrn_w_out, kv_norm_w, w_kv, rel_bias, diff_w_q, diff_lambda, diff_subln_w,
              diff_w_out, final_norm_w):
    lbs = jnp.cumsum(jax.nn.softmax(hgrn_lower_bounds.astype(jnp.float32), axis=0), axis=0)
    h = x
    k1 = k2 = v = None
    for l in range(DEPTH):
        h = h + FFN_RES * swiglu(rmsnorm(h, norm_w[l, 0]), ffn_w_in[l, 0], ffn_w_out[l, 0])
        hn = rmsnorm(h, norm_w[l, 1])
        if l < N_A_LAYERS:
            h = h + hgrn2_mixer(hn, hgrn_w_in[l], lbs[l], hgrn_gnorm_w[l], hgrn_w_out[l])
        else:
            j = l - N_A_LAYERS
            lambda_init = 0.8 - 0.6 * math.exp(-0.3 * l)
            h = h + diff_attention(hn, k1, k2, v, diff_w_q[j], diff_lambda[j], diff_subln_w[j],
                                   diff_w_out[j], rel_bias, lambda_init)
        h = h + FFN_RES * swiglu(rmsnorm(h, norm_w[l, 2]), ffn_w_in[l, 1], ffn_w_out[l, 1])
        if l == N_A_LAYERS - 1:
            k1, k2, v = shared_kv(h, kv_norm_w, w_kv)
    return rmsnorm(h, final_norm_w)
```

```python
import functools
import math

import numpy as np
import jax
import jax.numpy as jnp
from jax import lax
from jax.experimental import pallas as pl
from jax.experimental.pallas import tpu as pltpu

F32 = jnp.float32
BF16 = jnp.bfloat16

D_MODEL = 1024
D_FF = 2816
EPS = 1e-6
FFN_RES = 0.5
HEADS = 8
HEAD_W = D_MODEL // HEADS
HALF_W = HEAD_W // 2
HGRN_CHUNK = 64
N_BUCKETS = 32
MAX_DISTANCE = 128
LOG2E = math.log2(math.e)
NEG = -1e30

FFN_ROWS = 512
FFN_COLS = 256
HGRN_ROWS = 256
ATT_TILE = 512
VMEM_LIMIT = 56 * 1024 * 1024

NT_DIMS = (((1,), (1,)), ((), ()))
TN_DIMS = (((0,), (0,)), ((), ()))


def _dot(a, b):
    return jnp.dot(a, b, preferred_element_type=F32)


def _rms(x, w):
    return x * lax.rsqrt(jnp.mean(x * x, axis=-1, keepdims=True) + EPS) * w


def _silu(x):
    return x * jax.nn.sigmoid(x)


def _resident(shape):
    zeros = (0,) * len(shape)
    return pl.BlockSpec(shape, lambda *_: zeros, pipeline_mode=pl.Buffered(1))


def _ffn_kernel(*refs, prologue, epilogue):
    it = iter(refs)
    h_ref = next(it)
    if prologue == "attn_out":
        o_ref, wo_ref = next(it), next(it)
    nw_ref, wg_ref, wu_ref, wd_ref = next(it), next(it), next(it), next(it)
    if epilogue is not None:
        enw_ref = next(it)
    if epilogue in ("kv", "q"):
        ew_ref = next(it)
    outs = [next(it) for _ in range({None: 1, "final": 1, "q": 2, "kv": 3}[epilogue])]
    act_ref = next(it)

    h = h_ref[...]
    if prologue == "attn_out":
        h = h + _dot(o_ref[...], wo_ref[...])
    xn = _rms(h, nw_ref[...]).astype(BF16)
    for c in range(D_FF // FFN_COLS):
        sl = slice(c * FFN_COLS, (c + 1) * FFN_COLS)
        g = _dot(xn, wg_ref[:, sl])
        u = _dot(xn, wu_ref[:, sl])
        act_ref[:, sl] = (_silu(g) * u).astype(BF16)
    h = h + FFN_RES * _dot(act_ref[...], wd_ref[...])

    if epilogue == "final":
        outs[0][...] = _rms(h, enw_ref[...])
        return
    outs[0][...] = h
    if epilogue == "kv":
        kv = _dot(_rms(h, enw_ref[...]).astype(BF16), ew_ref[...])
        outs[1][...] = kv[:, :D_MODEL].astype(BF16)
        outs[2][0] = kv[:, D_MODEL:].T.astype(BF16)
    elif epilogue == "q":
        q = _dot(_rms(h, enw_ref[...]).astype(BF16), ew_ref[...])
        outs[1][0] = (q * (HALF_W ** -0.5 * LOG2E)).T.astype(BF16)


def _ffn(h, nw, wg, wu, wd, *, batch, prologue=None, pro_args=(), epilogue=None, epi_args=()):
    n = h.shape[0]
    seq = n // batch
    tm = FFN_ROWS
    per_seq = seq // tm
    row = pl.BlockSpec((tm, D_MODEL), lambda i: (i, 0))
    tr = pl.BlockSpec((1, D_MODEL, tm), lambda i: (i // per_seq, 0, i % per_seq))
    in_specs, args = [row], [h]
    if prologue == "attn_out":
        in_specs += [row, _resident((D_MODEL, D_MODEL))]
        args += list(pro_args)
    in_specs += [_resident((1, D_MODEL)), _resident((D_MODEL, D_FF)), _resident((D_MODEL, D_FF)),
                 _resident((D_FF, D_MODEL))]
    args += [nw, wg, wu, wd]
    if epilogue is not None:
        in_specs.append(_resident((1, D_MODEL)))
    if epilogue in ("kv", "q"):
        in_specs.append(_resident(epi_args[1].shape))
    args += list(epi_args)
    h_shape = jax.ShapeDtypeStruct((n, D_MODEL), F32)
    t_shape = jax.ShapeDtypeStruct((batch, D_MODEL, seq), BF16)
    if epilogue in (None, "final"):
        out_shape, out_specs = h_shape, row
    elif epilogue == "q":
        out_shape, out_specs = (h_shape, t_shape), (row, tr)
    else:
        out_shape = (h_shape, jax.ShapeDtypeStruct((n, D_MODEL), BF16), t_shape)
        out_specs = (row, row, tr)
    return pl.pallas_call(
        functools.partial(_ffn_kernel, prologue=prologue, epilogue=epilogue),
        out_shape=out_shape,
        grid=(n // tm,),
        in_specs=in_specs,
        out_specs=out_specs,
        scratch_shapes=[pltpu.VMEM((tm, D_FF), BF16)],
        compiler_params=pltpu.CompilerParams(
            dimension_semantics=("arbitrary",), vmem_limit_bytes=VMEM_LIMIT),
        name="ffn_" + str(prologue) + "_" + str(epilogue),
    )(*args)


def _hgrn_kernel(h_ref, nw_ref, win_ref, lbraw_ref, ltri_ref, gw_ref, wout_ref, out_ref,
                 st_ref, o_ref, og_ref):
    t = h_ref.shape[1]
    c = HGRN_CHUNK

    @pl.when(pl.program_id(1) == 0)
    def _():
        st_ref[...] = jnp.zeros_like(st_ref)

    h = h_ref[0]
    z = _dot(_rms(h, nw_ref[...]).astype(BF16), win_ref[...])
    q = _silu(z[:, :D_MODEL])
    lbr = lbraw_ref[...]
    e = jnp.exp(lbr - jnp.max(lbr, axis=0, keepdims=True))
    lb = e[0:1] / jnp.sum(e, axis=0, keepdims=True)
    f = lb + (1.0 - lb) * jax.nn.sigmoid(z[:, D_MODEL:2 * D_MODEL])
    k = 1.0 - f
    logf = jnp.log(f)
    hi = logf.astype(BF16)
    lo = (logf - hi.astype(F32)).astype(BF16)
    ltri = ltri_ref[...]
    bc = _dot(ltri, hi) + _dot(ltri, lo)
    v = z[:, 2 * D_MODEL:3 * D_MODEL]

    tril = lax.broadcasted_iota(jnp.int32, (c, c), 0) >= lax.broadcasted_iota(jnp.int32, (c, c), 1)
    for n in range(t // c):
        r = slice(n * c, (n + 1) * c)
        bcn = bc[r]
        b_last = bcn[c - 1:c]
        b_mid = bcn[c // 2 - 1:c // 2]
        qn, kn = q[r], k[r]
        qd = (qn * jnp.exp(bcn)).astype(BF16)
        kd = (kn * jnp.exp(b_last - bcn)).astype(BF16)
        qi = (qn * jnp.exp(bcn - b_mid)).astype(BF16)
        ki = (kn * jnp.exp(b_mid - bcn)).astype(BF16)
        dec = jnp.exp(b_last)
        vb = v[r].astype(BF16)
        for hh in range(HEADS):
            cs = slice(hh * HEAD_W, (hh + 1) * HEAD_W)
            a = lax.dot_general(qi[:, cs], ki[:, cs], NT_DIMS, preferred_element_type=F32)
            a = jnp.where(tril, a, 0.0).astype(BF16)
            st = st_ref[hh]
            o_ref[r, cs] = _dot(a, vb[:, cs]) + lax.dot_general(
                qd[:, cs], st.astype(BF16), NT_DIMS, preferred_element_type=F32)
            ut = lax.dot_general(vb[:, cs], kd[:, cs], TN_DIMS, preferred_element_type=F32)
            st_ref[hh] = st * dec[:, cs] + ut

    gate = _silu(z[:, 3 * D_MODEL:])
    gw = gw_ref[...]
    for hh in range(HEADS):
        cs = slice(hh * HEAD_W, (hh + 1) * HEAD_W)
        og_ref[:, cs] = (_rms(o_ref[:, cs], gw) * gate[:, cs]).astype(BF16)
    out_ref[0] = h + _dot(og_ref[...], wout_ref[...])


def _hgrn(h3, nw, win, lbraw, gw, wout):
    b, s, _ = h3.shape
    t = HGRN_ROWS
    idx = np.arange(t)
    ltri = jnp.asarray((idx[:, None] // HGRN_CHUNK == idx[None, :] // HGRN_CHUNK)
                       & (idx[None, :] <= idx[:, None]), BF16)
    blk = pl.BlockSpec((1, t, D_MODEL), lambda i, j: (i, j, 0))
    return pl.pallas_call(
        _hgrn_kernel,
        out_shape=jax.ShapeDtypeStruct(h3.shape, F32),
        grid=(b, s // t),
        in_specs=[blk, _resident((1, D_MODEL)), _resident(win.shape), _resident(lbraw.shape),
                  _resident((t, t)), _resident((1, HEAD_W)), _resident((D_MODEL, D_MODEL))],
        out_specs=blk,
        scratch_shapes=[pltpu.VMEM((HEADS, HEAD_W, HEAD_W), F32),
                        pltpu.VMEM((t, D_MODEL), F32),
                        pltpu.VMEM((t, D_MODEL), BF16)],
        compiler_params=pltpu.CompilerParams(
            dimension_semantics=("arbitrary", "arbitrary"), vmem_limit_bytes=VMEM_LIMIT),
        name="hgrn_layer",
    )(h3, nw, win, lbraw, ltri, gw, wout)


def _bucket_starts():
    n = np.arange(4 * MAX_DISTANCE, dtype=np.int64)
    max_exact = N_BUCKETS // 2
    nf = np.maximum(n, 1).astype(np.float32)
    large = max_exact + (np.log(nf / np.float32(max_exact)) / np.float32(math.log(MAX_DISTANCE / max_exact))
                         * np.float32(N_BUCKETS - max_exact)).astype(np.int32)
    bucket = np.where(n < max_exact, n, np.minimum(large, N_BUCKETS - 1))
    assert np.all(np.diff(bucket) >= 0) and bucket[-1] == N_BUCKETS - 1
    return [int(np.argmax(bucket >= b)) for b in range(N_BUCKETS)]


def _bias_kernel(tab_ref, out_ref, *, starts):
    hh, w = pl.program_id(0), pl.program_id(1)
    t = out_ref.shape[2]
    d = (lax.broadcasted_iota(jnp.int32, (t, t), 1) - lax.broadcasted_iota(jnp.int32, (t, t), 0)) + w * t
    val = jnp.full((t, t), tab_ref[0, hh], F32)
    for b in range(1, N_BUCKETS):
        val = jnp.where(d >= starts[b], tab_ref[b, hh], val)
    out_ref[0, 0] = jnp.where(d >= 0, val * LOG2E, NEG)


def _bias_tiles(rel_bias, t):
    return pl.pallas_call(
        functools.partial(_bias_kernel, starts=_bucket_starts()),
        out_shape=jax.ShapeDtypeStruct((HEADS, 2, t, t), F32),
        grid=(HEADS, 2),
        in_specs=[pl.BlockSpec(memory_space=pltpu.SMEM)],
        out_specs=pl.BlockSpec((1, 1, t, t), lambda i, j: (i, j, 0, 0)),
        compiler_params=pltpu.CompilerParams(dimension_semantics=("arbitrary", "arbitrary")),
        name="t5_bias_tiles",
    )(rel_bias)


def _attn_kernel(far_ref, qt_ref, kk_ref, vt_ref, bias_ref, lam_ref, sw_ref, o_ref,
                 acc1, acc2, m1, l1, m2, l2, *, lambda_init):
    hh, qi = pl.program_id(1), pl.program_id(2)
    t = qt_ref.shape[2]
    qt = qt_ref[0]
    row = lax.broadcasted_iota(jnp.int32, qt.shape, 0)
    qa = jnp.where(row < HALF_W, qt, jnp.zeros_like(qt))
    qb = jnp.where(row >= HALF_W, qt, jnp.zeros_like(qt))
    for acc, m, l in ((acc1, m1, l1), (acc2, m2, l2)):
        acc[...] = jnp.zeros_like(acc)
        m[...] = jnp.full_like(m, NEG)
        l[...] = jnp.zeros_like(l)

    def block(ki, bias):
        start = pl.multiple_of(ki * t, t)
        kt = kk_ref[0, pl.ds(start, t), :]
        vt = vt_ref[0, :, pl.ds(start, t)]
        for qx, acc, m, l in ((qa, acc1, m1, l1), (qb, acc2, m2, l2)):
            s = _dot(kt, qx) + bias
            m_old = m[...]
            m_new = jnp.maximum(m_old, jnp.max(s, axis=0, keepdims=True))
            alpha = jnp.exp2(m_old - m_new)
            p = jnp.exp2(s - m_new)
            l[...] = alpha * l[...] + jnp.sum(p, axis=0, keepdims=True)
            acc[...] = alpha * acc[...] + _dot(vt, p.astype(BF16))
            m[...] = m_new

    far = far_ref[hh]

    def far_body(ki, carry):
        block(ki, far)
        return carry

    lax.fori_loop(0, jnp.maximum(qi - 1, 0), far_body, 0)

    @pl.when(qi >= 1)
    def _():
        block(qi - 1, bias_ref[0, 1])

    block(qi, bias_ref[0, 0])

    lp = lam_ref[...]
    lam = (jnp.exp(jnp.sum(lp[0:1] * lp[1:2], axis=1, keepdims=True))
           - jnp.exp(jnp.sum(lp[2:3] * lp[3:4], axis=1, keepdims=True)) + lambda_init)
    ot = acc1[...] * (1.0 / l1[...]) - lam * (acc2[...] * (1.0 / l2[...]))
    ms = jnp.mean(ot * ot, axis=0, keepdims=True)
    y = ot * lax.rsqrt(ms + EPS) * sw_ref[...] * (1.0 - lambda_init)
    o_ref[0] = y.T.astype(BF16)


def _attention(far, qt, kk3, vt, bias, lam_p, sw_col, *, lambda_init):
    b, s, _ = kk3.shape
    t = ATT_TILE
    return pl.pallas_call(
        functools.partial(_attn_kernel, lambda_init=lambda_init),
        out_shape=jax.ShapeDtypeStruct((b, s, D_MODEL), BF16),
        grid=(b, HEADS, s // t),
        in_specs=[
            pl.BlockSpec(memory_space=pltpu.SMEM),
            pl.BlockSpec((1, HEAD_W, t), lambda i, j, k: (i, j, k)),
            pl.BlockSpec((1, s, HEAD_W), lambda i, j, k: (i, 0, j)),
            pl.BlockSpec((1, HEAD_W, s), lambda i, j, k: (i, j, 0)),
            pl.BlockSpec((1, 2, t, t), lambda i, j, k: (j, 0, 0, 0)),
            _resident(lam_p.shape),
            _resident((HEAD_W, 1)),
        ],
        out_specs=pl.BlockSpec((1, t, HEAD_W), lambda i, j, k: (i, k, j)),
        scratch_shapes=[pltpu.VMEM((HEAD_W, t), F32), pltpu.VMEM((HEAD_W, t), F32),
                        pltpu.VMEM((1, t), F32), pltpu.VMEM((1, t), F32),
                        pltpu.VMEM((1, t), F32), pltpu.VMEM((1, t), F32)],
        compiler_params=pltpu.CompilerParams(
            dimension_semantics=("arbitrary", "arbitrary", "arbitrary"), vmem_limit_bytes=VMEM_LIMIT),
        name="diff_attention",
    )(far, qt, kk3, vt, bias, lam_p, sw_col)


def _head_major(w):
    d_in = w.shape[0]
    return w.reshape(d_in, 2, HEADS, HALF_W).transpose(0, 2, 1, 3).reshape(d_in, D_MODEL)


def kernel(x, norm_w, ffn_w_in, ffn_w_out, hgrn_w_in, hgrn_lower_bounds, hgrn_gnorm_w, hgrn_w_out,
           kv_norm_w, w_kv, rel_bias, diff_w_q, diff_lambda, diff_subln_w, diff_w_out, final_norm_w):
    b, s, d = x.shape
    assert d == D_MODEL and s % ATT_TILE == 0 and s % FFN_ROWS == 0 and s % HGRN_ROWS == 0
    assert ATT_TILE >= MAX_DISTANCE
    n = b * s

    def ffn_w(l, j):
        w_in = ffn_w_in[l, j].astype(BF16)
        return (norm_w[l, 2 * j].reshape(1, d), w_in[:, :D_FF], w_in[:, D_FF:], ffn_w_out[l, j].astype(BF16))

    wkv = jnp.concatenate([_head_major(w_kv[:, :D_MODEL]), w_kv[:, D_MODEL:]], axis=1).astype(BF16)
    wq = _head_major(diff_w_q[0]).astype(BF16)
    lambda_init = 0.8 - 0.6 * math.exp(-0.3 * 1)

    bias = _bias_tiles(rel_bias, ATT_TILE)
    far = rel_bias[N_BUCKETS - 1] * LOG2E

    h = _ffn(x.reshape(n, d), *ffn_w(0, 0), batch=b)
    h = _hgrn(h.reshape(b, s, d), norm_w[0, 1].reshape(1, d), hgrn_w_in[0].astype(BF16),
              hgrn_lower_bounds, hgrn_gnorm_w[0].reshape(1, HEAD_W), hgrn_w_out[0].astype(BF16))
    h, kk, vt = _ffn(h.reshape(n, d), *ffn_w(0, 1), batch=b, epilogue="kv",
                     epi_args=(kv_norm_w.reshape(1, d), wkv))
    h, qt = _ffn(h, *ffn_w(1, 0), batch=b, epilogue="q", epi_args=(norm_w[1, 1].reshape(1, d), wq))
    o = _attention(far, qt, kk.reshape(b, s, d), vt, bias, diff_lambda[0],
                   diff_subln_w[0].reshape(HEAD_W, 1), lambda_init=lambda_init)
    out = _ffn(h, *ffn_w(1, 1), batch=b, prologue="attn_out",
               pro_args=(o.reshape(n, d), diff_w_out[0].astype(BF16)),
               epilogue="final", epi_args=(final_norm_w.reshape(1, d),))
    return out.reshape(b, s, d)
```

```python
import functools
import math

import numpy as np
import jax
import jax.numpy as jnp
from jax import lax
from jax.experimental import pallas as pl
from jax.experimental.pallas import tpu as pltpu

F32 = jnp.float32
BF16 = jnp.bfloat16

D_MODEL = 1024
D_FF = 2816
EPS = 1e-6
FFN_RES = 0.5
HEADS = 8
HEAD_W = D_MODEL // HEADS
HALF_W = HEAD_W // 2
HGRN_CHUNK = 64
N_BUCKETS = 32
MAX_DISTANCE = 128
LOG2E = math.log2(math.e)
NEG = -1e30

FFN_ROWS = 512
FFN_COLS = 256
HGRN_ROWS = 256
ATT_TILE = 512
VMEM_LIMIT = 56 * 1024 * 1024

NT_DIMS = (((1,), (1,)), ((), ()))
TN_DIMS = (((0,), (0,)), ((), ()))


def _dot(a, b):
    return jnp.dot(a, b, preferred_element_type=F32)


def _rms(x, w):
    return x * lax.rsqrt(jnp.mean(x * x, axis=-1, keepdims=True) + EPS) * w


def _silu(x):
    return x * jax.nn.sigmoid(x)


def _resident(shape):
    zeros = (0,) * len(shape)
    return pl.BlockSpec(shape, lambda *_: zeros, pipeline_mode=pl.Buffered(1))


def _ffn_kernel(*refs, prologue, epilogue):
    it = iter(refs)
    h_ref = next(it)
    if prologue == "attn_out":
        o_ref, wo_ref = next(it), next(it)
    nw_ref, wg_ref, wu_ref, wd_ref = next(it), next(it), next(it), next(it)
    if epilogue is not None:
        enw_ref = next(it)
    if epilogue in ("kv", "q"):
        ew_ref = next(it)
    outs = [next(it) for _ in range({None: 1, "final": 1, "q": 2, "kv": 3}[epilogue])]
    act_ref = next(it)

    h = h_ref[...]
    if prologue == "attn_out":
        h = h + _dot(o_ref[...], wo_ref[...])
    xn = _rms(h, nw_ref[...]).astype(BF16)
    for c in range(D_FF // FFN_COLS):
        sl = slice(c * FFN_COLS, (c + 1) * FFN_COLS)
        g = _dot(xn, wg_ref[:, sl])
        u = _dot(xn, wu_ref[:, sl])
        act_ref[:, sl] = (_silu(g) * u).astype(BF16)
    h = h + FFN_RES * _dot(act_ref[...], wd_ref[...])

    if epilogue == "final":
        outs[0][...] = _rms(h, enw_ref[...])
        return
    outs[0][...] = h
    if epilogue == "kv":
        kv = _dot(_rms(h, enw_ref[...]).astype(BF16), ew_ref[...])
        outs[1][...] = kv[:, :D_MODEL].astype(BF16)
        outs[2][0] = kv[:, D_MODEL:].T.astype(BF16)
    elif epilogue == "q":
        q = _dot(_rms(h, enw_ref[...]).astype(BF16), ew_ref[...])
        outs[1][0] = (q * (HALF_W ** -0.5 * LOG2E)).T.astype(BF16)


def _ffn(h, nw, wg, wu, wd, *, batch, prologue=None, pro_args=(), epilogue=None, epi_args=()):
    n = h.shape[0]
    seq = n // batch
    tm = FFN_ROWS
    per_seq = seq // tm
    row = pl.BlockSpec((tm, D_MODEL), lambda i: (i, 0))
    tr = pl.BlockSpec((1, D_MODEL, tm), lambda i: (i // per_seq, 0, i % per_seq))
    in_specs, args = [row], [h]
    if prologue == "attn_out":
        in_specs += [row, _resident((D_MODEL, D_MODEL))]
        args += list(pro_args)
    in_specs += [_resident((1, D_MODEL)), _resident((D_MODEL, D_FF)), _resident((D_MODEL, D_FF)),
                 _resident((D_FF, D_MODEL))]
    args += [nw, wg, wu, wd]
    if epilogue is not None:
        in_specs.append(_resident((1, D_MODEL)))
    if epilogue in ("kv", "q"):
        in_specs.append(_resident(epi_args[1].shape))
    args += list(epi_args)
    h_shape = jax.ShapeDtypeStruct((n, D_MODEL), F32)
    t_shape = jax.ShapeDtypeStruct((batch, D_MODEL, seq), BF16)
    if epilogue in (None, "final"):
        out_shape, out_specs = h_shape, row
    elif epilogue == "q":
        out_shape, out_specs = (h_shape, t_shape), (row, tr)
    else:
        out_shape = (h_shape, jax.ShapeDtypeStruct((n, D_MODEL), BF16), t_shape)
        out_specs = (row, row, tr)
    return pl.pallas_call(
        functools.partial(_ffn_kernel, prologue=prologue, epilogue=epilogue),
        out_shape=out_shape,
        grid=(n // tm,),
        in_specs=in_specs,
        out_specs=out_specs,
        scratch_shapes=[pltpu.VMEM((tm, D_FF), BF16)],
        compiler_params=pltpu.CompilerParams(
            dimension_semantics=("arbitrary",), vmem_limit_bytes=VMEM_LIMIT),
        name="ffn_" + str(prologue) + "_" + str(epilogue),
    )(*args)


def _hgrn_kernel(h_ref, nw_ref, win_ref, lbraw_ref, ltri_ref, gw_ref, wout_ref, out_ref,
                 st_ref, o_ref, og_ref):
    t = h_ref.shape[1]
    c = HGRN_CHUNK

    @pl.when(pl.program_id(1) == 0)
    def _():
        st_ref[...] = jnp.zeros_like(st_ref)

    h = h_ref[0]
    z = _dot(_rms(h, nw_ref[...]).astype(BF16), win_ref[...])
    q = _silu(z[:, :D_MODEL])
    lbr = lbraw_ref[...]
    e = jnp.exp(lbr - jnp.max(lbr, axis=0, keepdims=True))
    lb = e[0:1] / jnp.sum(e, axis=0, keepdims=True)
    f = lb + (1.0 - lb) * jax.nn.sigmoid(z[:, D_MODEL:2 * D_MODEL])
    k = 1.0 - f
    logf = jnp.log(f)
    hi = logf.astype(BF16)
    lo = (logf - hi.astype(F32)).astype(BF16)
    ltri = ltri_ref[...]
    bc = _dot(ltri, hi) + _dot(ltri, lo)
    v = z[:, 2 * D_MODEL:3 * D_MODEL]

    tril = lax.broadcasted_iota(jnp.int32, (c, c), 0) >= lax.broadcasted_iota(jnp.int32, (c, c), 1)
    for n in range(t // c):
        r = slice(n * c, (n + 1) * c)
        bcn = bc[r]
        b_last = bcn[c - 1:c]
        b_mid = bcn[c // 2 - 1:c // 2]
        qn, kn = q[r], k[r]
        qd = (qn * jnp.exp(bcn)).astype(BF16)
        kd = (kn * jnp.exp(b_last - bcn)).astype(BF16)
        qi = (qn * jnp.exp(bcn - b_mid)).astype(BF16)
        ki = (kn * jnp.exp(b_mid - bcn)).astype(BF16)
        dec = jnp.exp(b_last)
        vb = v[r].astype(BF16)
        for hh in range(HEADS):
            cs = slice(hh * HEAD_W, (hh + 1) * HEAD_W)
            a = lax.dot_general(qi[:, cs], ki[:, cs], NT_DIMS, preferred_element_type=F32)
            a = jnp.where(tril, a, 0.0).astype(BF16)
            st = st_ref[hh]
            o_ref[r, cs] = _dot(a, vb[:, cs]) + lax.dot_general(
                qd[:, cs], st.astype(BF16), NT_DIMS, preferred_element_type=F32)
            ut = lax.dot_general(vb[:, cs], kd[:, cs], TN_DIMS, preferred_element_type=F32)
            st_ref[hh] = st * dec[:, cs] + ut

    gate = _silu(z[:, 3 * D_MODEL:])
    gw = gw_ref[...]
    for hh in range(HEADS):
        cs = slice(hh * HEAD_W, (hh + 1) * HEAD_W)
        og_ref[:, cs] = (_rms(o_ref[:, cs], gw) * gate[:, cs]).astype(BF16)
    out_ref[0] = h + _dot(og_ref[...], wout_ref[...])


def _hgrn(h3, nw, win, lbraw, gw, wout):
    b, s, _ = h3.shape
    t = HGRN_ROWS
    idx = np.arange(t)
    ltri = jnp.asarray((idx[:, None] // HGRN_CHUNK == idx[None, :] // HGRN_CHUNK)
                       & (idx[None, :] <= idx[:, None]), BF16)
    blk = pl.BlockSpec((1, t, D_MODEL), lambda i, j: (i, j, 0))
    return pl.pallas_call(
        _hgrn_kernel,
        out_shape=jax.ShapeDtypeStruct(h3.shape, F32),
        grid=(b, s // t),
        in_specs=[blk, _resident((1, D_MODEL)), _resident(win.shape), _resident(lbraw.shape),
                  _resident((t, t)), _resident((1, HEAD_W)), _resident((D_MODEL, D_MODEL))],
        out_specs=blk,
        scratch_shapes=[pltpu.VMEM((HEADS, HEAD_W, HEAD_W), F32),
                        pltpu.VMEM((t, D_MODEL), F32),
                        pltpu.VMEM((t, D_MODEL), BF16)],
        compiler_params=pltpu.CompilerParams(
            dimension_semantics=("arbitrary", "arbitrary"), vmem_limit_bytes=VMEM_LIMIT),
        name="hgrn_layer",
    )(h3, nw, win, lbraw, ltri, gw, wout)


def _bucket_starts():
    n = np.arange(4 * MAX_DISTANCE, dtype=np.int64)
    max_exact = N_BUCKETS // 2
    nf = np.maximum(n, 1).astype(np.float32)
    large = max_exact + (np.log(nf / np.float32(max_exact)) / np.float32(math.log(MAX_DISTANCE / max_exact))
                         * np.float32(N_BUCKETS - max_exact)).astype(np.int32)
    bucket = np.where(n < max_exact, n, np.minimum(large, N_BUCKETS - 1))
    assert np.all(np.diff(bucket) >= 0) and bucket[-1] == N_BUCKETS - 1
    return [int(np.argmax(bucket >= b)) for b in range(N_BUCKETS)]


def _bias_kernel(tab_ref, out_ref, *, starts):
    hh, w = pl.program_id(0), pl.program_id(1)
    t = out_ref.shape[2]
    d = (lax.broadcasted_iota(jnp.int32, (t, t), 1) - lax.broadcasted_iota(jnp.int32, (t, t), 0)) + w * t
    val = jnp.full((t, t), tab_ref[0, hh], F32)
    for b in range(1, N_BUCKETS):
        val = jnp.where(d >= starts[b], tab_ref[b, hh], val)
    out_ref[0, 0] = jnp.where(d >= 0, (val - tab_ref[N_BUCKETS - 1, hh]) * LOG2E, NEG)


def _bias_tiles(rel_bias, t):
    return pl.pallas_call(
        functools.partial(_bias_kernel, starts=_bucket_starts()),
        out_shape=jax.ShapeDtypeStruct((HEADS, 2, t, t), F32),
        grid=(HEADS, 2),
        in_specs=[pl.BlockSpec(memory_space=pltpu.SMEM)],
        out_specs=pl.BlockSpec((1, 1, t, t), lambda i, j: (i, j, 0, 0)),
        compiler_params=pltpu.CompilerParams(dimension_semantics=("arbitrary", "arbitrary")),
        name="t5_bias_tiles",
    )(rel_bias)


def _attn_kernel(qt_ref, kk_ref, vt_ref, bias_ref, lam_ref, sw_ref, o_ref,
                 sa_ref, sb_ref, acc_ref, m_ref, l_ref, *, lambda_init):
    qi = pl.program_id(2)
    t = qt_ref.shape[2]
    qt = qt_ref[0]
    row = lax.broadcasted_iota(jnp.int32, qt.shape, 0)
    qx = (jnp.where(row < HALF_W, qt, jnp.zeros_like(qt)),
          jnp.where(row >= HALF_W, qt, jnp.zeros_like(qt)))
    acc_ref[...] = jnp.zeros_like(acc_ref)
    m_ref[...] = jnp.full_like(m_ref, NEG)
    l_ref[...] = jnp.zeros_like(l_ref)

    def scores(ki, dst):
        kt = kk_ref[0, pl.ds(pl.multiple_of(ki * t, t), t), :]
        for mp in range(2):
            dst[mp] = _dot(kt, qx[mp])

    def accumulate(ki, src, bias):
        vt = vt_ref[0, :, pl.ds(pl.multiple_of(ki * t, t), t)]
        for mp in range(2):
            s = src[mp]
            if bias is not None:
                s = s + bias
            m_old = m_ref[mp]
            m_new = jnp.maximum(m_old, jnp.max(s, axis=0, keepdims=True))
            alpha = jnp.exp2(m_old - m_new)
            p = jnp.exp2(s - m_new)
            l_ref[mp] = alpha * l_ref[mp] + jnp.sum(p, axis=0, keepdims=True)
            acc_ref[mp] = alpha * acc_ref[mp] + _dot(vt, p.astype(BF16))
            m_ref[mp] = m_new

    n_far = jnp.maximum(qi - 1, 0)
    scores(0, sa_ref)

    def far_pair(j, carry):
        k0 = 2 * j
        scores(k0 + 1, sb_ref)
        accumulate(k0, sa_ref, None)
        scores(k0 + 2, sa_ref)
        accumulate(k0 + 1, sb_ref, None)
        return carry

    lax.fori_loop(0, n_far // 2, far_pair, 0)

    @pl.when(qi == 0)
    def _():
        accumulate(0, sa_ref, bias_ref[0, 0])

    @pl.when((qi >= 1) & (n_far % 2 == 0))
    def _():
        scores(qi, sb_ref)
        accumulate(qi - 1, sa_ref, bias_ref[0, 1])
        accumulate(qi, sb_ref, bias_ref[0, 0])

    @pl.when((qi >= 1) & (n_far % 2 == 1))
    def _():
        scores(qi - 1, sb_ref)
        accumulate(qi - 2, sa_ref, None)
        scores(qi, sa_ref)
        accumulate(qi - 1, sb_ref, bias_ref[0, 1])
        accumulate(qi, sa_ref, bias_ref[0, 0])

    lp = lam_ref[...]
    lam = (jnp.exp(jnp.sum(lp[0:1] * lp[1:2], axis=1, keepdims=True))
           - jnp.exp(jnp.sum(lp[2:3] * lp[3:4], axis=1, keepdims=True)) + lambda_init)
    ot = acc_ref[0] * (1.0 / l_ref[0]) - lam * (acc_ref[1] * (1.0 / l_ref[1]))
    ms = jnp.mean(ot * ot, axis=0, keepdims=True)
    y = ot * lax.rsqrt(ms + EPS) * sw_ref[...] * (1.0 - lambda_init)
    o_ref[0] = y.T.astype(BF16)


def _attention(qt, kk3, vt, bias, lam_p, sw_col, *, lambda_init):
    b, s, _ = kk3.shape
    t = ATT_TILE
    return pl.pallas_call(
        functools.partial(_attn_kernel, lambda_init=lambda_init),
        out_shape=jax.ShapeDtypeStruct((b, s, D_MODEL), BF16),
        grid=(b, HEADS, s // t),
        in_specs=[
            pl.BlockSpec((1, HEAD_W, t), lambda i, j, k: (i, j, k)),
            pl.BlockSpec((1, s, HEAD_W), lambda i, j, k: (i, 0, j)),
            pl.BlockSpec((1, HEAD_W, s), lambda i, j, k: (i, j, 0)),
            pl.BlockSpec((1, 2, t, t), lambda i, j, k: (j, 0, 0, 0)),
            _resident(lam_p.shape),
            _resident((HEAD_W, 1)),
        ],
        out_specs=pl.BlockSpec((1, t, HEAD_W), lambda i, j, k: (i, k, j)),
        scratch_shapes=[pltpu.VMEM((2, t, t), F32), pltpu.VMEM((2, t, t), F32),
                        pltpu.VMEM((2, HEAD_W, t), F32),
                        pltpu.VMEM((2, 1, t), F32), pltpu.VMEM((2, 1, t), F32)],
        compiler_params=pltpu.CompilerParams(
            dimension_semantics=("arbitrary", "arbitrary", "arbitrary"), vmem_limit_bytes=VMEM_LIMIT),
        name="diff_attention",
    )(qt, kk3, vt, bias, lam_p, sw_col)


def _head_major(w):
    d_in = w.shape[0]
    return w.reshape(d_in, 2, HEADS, HALF_W).transpose(0, 2, 1, 3).reshape(d_in, D_MODEL)


def kernel(x, norm_w, ffn_w_in, ffn_w_out, hgrn_w_in, hgrn_lower_bounds, hgrn_gnorm_w, hgrn_w_out,
           kv_norm_w, w_kv, rel_bias, diff_w_q, diff_lambda, diff_subln_w, diff_w_out, final_norm_w):
    b, s, d = x.shape
    assert d == D_MODEL and s % ATT_TILE == 0 and s % FFN_ROWS == 0 and s % HGRN_ROWS == 0
    assert ATT_TILE >= MAX_DISTANCE
    n = b * s

    def ffn_w(l, j):
        w_in = ffn_w_in[l, j].astype(BF16)
        return (norm_w[l, 2 * j].reshape(1, d), w_in[:, :D_FF], w_in[:, D_FF:], ffn_w_out[l, j].astype(BF16))

    wkv = jnp.concatenate([_head_major(w_kv[:, :D_MODEL]), w_kv[:, D_MODEL:]], axis=1).astype(BF16)
    wq = _head_major(diff_w_q[0]).astype(BF16)
    lambda_init = 0.8 - 0.6 * math.exp(-0.3 * 1)

    bias = _bias_tiles(rel_bias, ATT_TILE)

    h = _ffn(x.reshape(n, d), *ffn_w(0, 0), batch=b)
    h = _hgrn(h.reshape(b, s, d), norm_w[0, 1].reshape(1, d), hgrn_w_in[0].astype(BF16),
              hgrn_lower_bounds, hgrn_gnorm_w[0].reshape(1, HEAD_W), hgrn_w_out[0].astype(BF16))
    h, kk, vt = _ffn(h.reshape(n, d), *ffn_w(0, 1), batch=b, epilogue="kv",
                     epi_args=(kv_norm_w.reshape(1, d), wkv))
    h, qt = _ffn(h, *ffn_w(1, 0), batch=b, epilogue="q", epi_args=(norm_w[1, 1].reshape(1, d), wq))
    o = _attention(qt, kk.reshape(b, s, d), vt, bias, diff_lambda[0],
                   diff_subln_w[0].reshape(HEAD_W, 1), lambda_init=lambda_init)
    out = _ffn(h, *ffn_w(1, 1), batch=b, prologue="attn_out",
               pro_args=(o.reshape(n, d), diff_w_out[0].astype(BF16)),
               epilogue="final", epi_args=(final_norm_w.reshape(1, d),))
    return out.reshape(b, s, d)
```

```python
import functools
import math

import numpy as np
import jax
import jax.numpy as jnp
from jax import lax
from jax.experimental import pallas as pl
from jax.experimental.pallas import tpu as pltpu

F32 = jnp.float32
BF16 = jnp.bfloat16

D_MODEL = 1024
D_FF = 2816
EPS = 1e-6
FFN_RES = 0.5
HEADS = 8
HEAD_W = D_MODEL // HEADS
HALF_W = HEAD_W // 2
HGRN_CHUNK = 64
N_BUCKETS = 32
MAX_DISTANCE = 128
LOG2E = math.log2(math.e)
NEG = -1e30

FFN_ROWS = 512
FFN_COLS = 256
HGRN_ROWS = 256
ATT_KV = 512
ATT_Q = 1024
VMEM_LIMIT = 56 * 1024 * 1024

NT_DIMS = (((1,), (1,)), ((), ()))
TN_DIMS = (((0,), (0,)), ((), ()))


def _dot(a, b):
    return jnp.dot(a, b, preferred_element_type=F32)


def _rms(x, w):
    return x * lax.rsqrt(jnp.mean(x * x, axis=-1, keepdims=True) + EPS) * w


def _silu(x):
    return x * jax.nn.sigmoid(x)


def _resident(shape):
    zeros = (0,) * len(shape)
    return pl.BlockSpec(shape, lambda *_: zeros, pipeline_mode=pl.Buffered(1))


def _ffn_kernel(*refs, prologue, epilogue):
    it = iter(refs)
    h_ref = next(it)
    if prologue == "attn_out":
        o_ref, wo_ref = next(it), next(it)
    nw_ref, wg_ref, wu_ref, wd_ref = next(it), next(it), next(it), next(it)
    if epilogue is not None:
        enw_ref = next(it)
    if epilogue in ("kv", "q"):
        ew_ref = next(it)
    outs = [next(it) for _ in range({None: 1, "final": 1, "q": 2, "kv": 3}[epilogue])]
    act_ref = next(it)

    h = h_ref[...]
    if prologue == "attn_out":
        h = h + _dot(o_ref[...], wo_ref[...])
    xn = _rms(h, nw_ref[...]).astype(BF16)
    for c in range(D_FF // FFN_COLS):
        sl = slice(c * FFN_COLS, (c + 1) * FFN_COLS)
        g = _dot(xn, wg_ref[:, sl])
        u = _dot(xn, wu_ref[:, sl])
        act_ref[:, sl] = (_silu(g) * u).astype(BF16)
    h = h + FFN_RES * _dot(act_ref[...], wd_ref[...])

    if epilogue == "final":
        outs[0][...] = _rms(h, enw_ref[...])
        return
    outs[0][...] = h
    if epilogue == "kv":
        kv = _dot(_rms(h, enw_ref[...]).astype(BF16), ew_ref[...])
        outs[1][...] = kv[:, :D_MODEL].astype(BF16)
        outs[2][0] = kv[:, D_MODEL:].T.astype(BF16)
    elif epilogue == "q":
        q = _dot(_rms(h, enw_ref[...]).astype(BF16), ew_ref[...])
        outs[1][0] = (q * (HALF_W ** -0.5 * LOG2E)).T.astype(BF16)


def _ffn(h, nw, wg, wu, wd, *, batch, prologue=None, pro_args=(), epilogue=None, epi_args=()):
    n = h.shape[0]
    seq = n // batch
    tm = FFN_ROWS
    per_seq = seq // tm
    row = pl.BlockSpec((tm, D_MODEL), lambda i: (i, 0))
    tr = pl.BlockSpec((1, D_MODEL, tm), lambda i: (i // per_seq, 0, i % per_seq))
    in_specs, args = [row], [h]
    if prologue == "attn_out":
        in_specs += [row, _resident((D_MODEL, D_MODEL))]
        args += list(pro_args)
    in_specs += [_resident((1, D_MODEL)), _resident((D_MODEL, D_FF)), _resident((D_MODEL, D_FF)),
                 _resident((D_FF, D_MODEL))]
    args += [nw, wg, wu, wd]
    if epilogue is not None:
        in_specs.append(_resident((1, D_MODEL)))
    if epilogue in ("kv", "q"):
        in_specs.append(_resident(epi_args[1].shape))
    args += list(epi_args)
    h_shape = jax.ShapeDtypeStruct((n, D_MODEL), F32)
    t_shape = jax.ShapeDtypeStruct((batch, D_MODEL, seq), BF16)
    if epilogue in (None, "final"):
        out_shape, out_specs = h_shape, row
    elif epilogue == "q":
        out_shape, out_specs = (h_shape, t_shape), (row, tr)
    else:
        out_shape = (h_shape, jax.ShapeDtypeStruct((n, D_MODEL), BF16), t_shape)
        out_specs = (row, row, tr)
    return pl.pallas_call(
        functools.partial(_ffn_kernel, prologue=prologue, epilogue=epilogue),
        out_shape=out_shape,
        grid=(n // tm,),
        in_specs=in_specs,
        out_specs=out_specs,
        scratch_shapes=[pltpu.VMEM((tm, D_FF), BF16)],
        compiler_params=pltpu.CompilerParams(
            dimension_semantics=("arbitrary",), vmem_limit_bytes=VMEM_LIMIT),
        name="ffn_" + str(prologue) + "_" + str(epilogue),
    )(*args)


def _hgrn_kernel(h_ref, nw_ref, win_ref, lbraw_ref, ltri_ref, gw_ref, wout_ref, out_ref,
                 st_ref, o_ref, og_ref):
    t = h_ref.shape[1]
    c = HGRN_CHUNK

    @pl.when(pl.program_id(1) == 0)
    def _():
        st_ref[...] = jnp.zeros_like(st_ref)

    h = h_ref[0]
    z = _dot(_rms(h, nw_ref[...]).astype(BF16), win_ref[...])
    q = _silu(z[:, :D_MODEL])
    lbr = lbraw_ref[...]
    e = jnp.exp(lbr - jnp.max(lbr, axis=0, keepdims=True))
    lb = e[0:1] / jnp.sum(e, axis=0, keepdims=True)
    f = lb + (1.0 - lb) * jax.nn.sigmoid(z[:, D_MODEL:2 * D_MODEL])
    k = 1.0 - f
    logf = jnp.log(f)
    hi = logf.astype(BF16)
    lo = (logf - hi.astype(F32)).astype(BF16)
    ltri = ltri_ref[...]
    bc = _dot(ltri, hi) + _dot(ltri, lo)
    v = z[:, 2 * D_MODEL:3 * D_MODEL]

    tril = lax.broadcasted_iota(jnp.int32, (c, c), 0) >= lax.broadcasted_iota(jnp.int32, (c, c), 1)
    for n in range(t // c):
        r = slice(n * c, (n + 1) * c)
        bcn = bc[r]
        b_last = bcn[c - 1:c]
        b_mid = bcn[c // 2 - 1:c // 2]
        qn, kn = q[r], k[r]
        qd = (qn * jnp.exp(bcn)).astype(BF16)
        kd = (kn * jnp.exp(b_last - bcn)).astype(BF16)
        qi = (qn * jnp.exp(bcn - b_mid)).astype(BF16)
        ki = (kn * jnp.exp(b_mid - bcn)).astype(BF16)
        dec = jnp.exp(b_last)
        vb = v[r].astype(BF16)
        for hh in range(HEADS):
            cs = slice(hh * HEAD_W, (hh + 1) * HEAD_W)
            a = lax.dot_general(qi[:, cs], ki[:, cs], NT_DIMS, preferred_element_type=F32)
            a = jnp.where(tril, a, 0.0).astype(BF16)
            st = st_ref[hh]
            o_ref[r, cs] = _dot(a, vb[:, cs]) + lax.dot_general(
                qd[:, cs], st.astype(BF16), NT_DIMS, preferred_element_type=F32)
            ut = lax.dot_general(vb[:, cs], kd[:, cs], TN_DIMS, preferred_element_type=F32)
            st_ref[hh] = st * dec[:, cs] + ut

    gate = _silu(z[:, 3 * D_MODEL:])
    gw = gw_ref[...]
    for hh in range(HEADS):
        cs = slice(hh * HEAD_W, (hh + 1) * HEAD_W)
        og_ref[:, cs] = (_rms(o_ref[:, cs], gw) * gate[:, cs]).astype(BF16)
    out_ref[0] = h + _dot(og_ref[...], wout_ref[...])


def _hgrn(h3, nw, win, lbraw, gw, wout):
    b, s, _ = h3.shape
    t = HGRN_ROWS
    idx = np.arange(t)
    ltri = jnp.asarray((idx[:, None] // HGRN_CHUNK == idx[None, :] // HGRN_CHUNK)
                       & (idx[None, :] <= idx[:, None]), BF16)
    blk = pl.BlockSpec((1, t, D_MODEL), lambda i, j: (i, j, 0))
    return pl.pallas_call(
        _hgrn_kernel,
        out_shape=jax.ShapeDtypeStruct(h3.shape, F32),
        grid=(b, s // t),
        in_specs=[blk, _resident((1, D_MODEL)), _resident(win.shape), _resident(lbraw.shape),
                  _resident((t, t)), _resident((1, HEAD_W)), _resident((D_MODEL, D_MODEL))],
        out_specs=blk,
        scratch_shapes=[pltpu.VMEM((HEADS, HEAD_W, HEAD_W), F32),
                        pltpu.VMEM((t, D_MODEL), F32),
                        pltpu.VMEM((t, D_MODEL), BF16)],
        compiler_params=pltpu.CompilerParams(
            dimension_semantics=("arbitrary", "arbitrary"), vmem_limit_bytes=VMEM_LIMIT),
        name="hgrn_layer",
    )(h3, nw, win, lbraw, ltri, gw, wout)


def _bucket_starts():
    n = np.arange(4 * MAX_DISTANCE, dtype=np.int64)
    max_exact = N_BUCKETS // 2
    nf = np.maximum(n, 1).astype(np.float32)
    large = max_exact + (np.log(nf / np.float32(max_exact)) / np.float32(math.log(MAX_DISTANCE / max_exact))
                         * np.float32(N_BUCKETS - max_exact)).astype(np.int32)
    bucket = np.where(n < max_exact, n, np.minimum(large, N_BUCKETS - 1))
    assert np.all(np.diff(bucket) >= 0) and bucket[-1] == N_BUCKETS - 1
    return [int(np.argmax(bucket >= b)) for b in range(N_BUCKETS)]


def _bias_kernel(tab_ref, out_ref, *, starts):
    hh, w = pl.program_id(0), pl.program_id(1)
    tk, tq = out_ref.shape[2:]
    d = (lax.broadcasted_iota(jnp.int32, (tk, tq), 1) - lax.broadcasted_iota(jnp.int32, (tk, tq), 0)) + (1 - w) * tk
    val = jnp.full((tk, tq), tab_ref[0, hh], F32)
    for b in range(1, N_BUCKETS):
        val = jnp.where(d >= starts[b], tab_ref[b, hh], val)
    out_ref[0, 0] = jnp.where(d >= 0, (val - tab_ref[N_BUCKETS - 1, hh]) * LOG2E, NEG)


def _bias_tiles(rel_bias):
    n_near = ATT_Q // ATT_KV + 1
    return pl.pallas_call(
        functools.partial(_bias_kernel, starts=_bucket_starts()),
        out_shape=jax.ShapeDtypeStruct((HEADS, n_near, ATT_KV, ATT_Q), F32),
        grid=(HEADS, n_near),
        in_specs=[pl.BlockSpec(memory_space=pltpu.SMEM)],
        out_specs=pl.BlockSpec((1, 1, ATT_KV, ATT_Q), lambda i, j: (i, j, 0, 0)),
        compiler_params=pltpu.CompilerParams(dimension_semantics=("arbitrary", "arbitrary")),
        name="t5_bias_tiles",
    )(rel_bias)


def _attn_kernel(qt_ref, kk_ref, vt_ref, bias_ref, lam_ref, sw_ref, o_ref,
                 sa_ref, sb_ref, acc_ref, m_ref, l_ref, *, lambda_init):
    tk, tq = ATT_KV, ATT_Q
    r = tq // tk
    nq = kk_ref.shape[1] // tq
    n_steps = r * nq * (nq + 1) // 2
    row = lax.broadcasted_iota(jnp.int32, (HEAD_W, tq), 0)

    def reset():
        acc_ref[...] = jnp.zeros_like(acc_ref)
        m_ref[...] = jnp.full_like(m_ref, NEG)
        l_ref[...] = jnp.zeros_like(l_ref)

    def scores(qi, ki, dst):
        qt = qt_ref[0, :, pl.ds(pl.multiple_of(qi * tq, tq), tq)]
        kt = kk_ref[0, pl.ds(pl.multiple_of(ki * tk, tk), tk), :]
        dst[0] = _dot(kt, jnp.where(row < HALF_W, qt, jnp.zeros_like(qt)))
        dst[1] = _dot(kt, jnp.where(row >= HALF_W, qt, jnp.zeros_like(qt)))

    def accumulate(ki, src, near):
        vt = vt_ref[0, :, pl.ds(pl.multiple_of(ki * tk, tk), tk)]
        for mp in range(2):
            s = src[mp]
            if near is not None:
                s = s + bias_ref[0, near]
            m_old = m_ref[mp]
            m_new = jnp.maximum(m_old, jnp.max(s, axis=0, keepdims=True))
            alpha = jnp.exp2(m_old - m_new)
            p = jnp.exp2(s - m_new)
            l_ref[mp] = alpha * l_ref[mp] + jnp.sum(p, axis=0, keepdims=True)
            acc_ref[mp] = alpha * acc_ref[mp] + _dot(vt, p.astype(BF16))
            m_ref[mp] = m_new

    def finalize(qi):
        lp = lam_ref[...]
        lam = (jnp.exp(jnp.sum(lp[0:1] * lp[1:2], axis=1, keepdims=True))
               - jnp.exp(jnp.sum(lp[2:3] * lp[3:4], axis=1, keepdims=True)) + lambda_init)
        ot = acc_ref[0] * (1.0 / l_ref[0]) - lam * (acc_ref[1] * (1.0 / l_ref[1]))
        ms = jnp.mean(ot * ot, axis=0, keepdims=True)
        y = ot * lax.rsqrt(ms + EPS) * sw_ref[...] * (1.0 - lambda_init)
        o_ref[0, pl.ds(pl.multiple_of(qi * tq, tq), tq), :] = y.T.astype(BF16)
        reset()

    def step(qi, ki, src, dst):
        last = ki == r * qi + r - 1
        qn = jnp.where(last, jnp.minimum(qi + 1, nq - 1), qi)
        kn = jnp.where(last, 0, ki + 1)
        near = ki - (r * qi - 1)
        far = near < 0

        @pl.when(far)
        def _():
            scores(qn, kn, dst)
            accumulate(ki, src, None)

        @pl.when(jnp.logical_not(far))
        def _():
            scores(qn, kn, dst)
            accumulate(ki, src, near)

        @pl.when(last)
        def _():
            finalize(qi)

        return qn, kn

    reset()
    scores(0, 0, sa_ref)

    def pair(_, carry):
        qi, ki = step(*carry, sa_ref, sb_ref)
        return step(qi, ki, sb_ref, sa_ref)

    carry = lax.fori_loop(0, n_steps // 2, pair, (jnp.int32(0), jnp.int32(0)))
    if n_steps % 2:
        step(*carry, sa_ref, sb_ref)


def _attention(qt, kk3, vt, bias, lam_p, sw_col, *, lambda_init):
    b, s, _ = kk3.shape
    tk, tq = ATT_KV, ATT_Q
    return pl.pallas_call(
        functools.partial(_attn_kernel, lambda_init=lambda_init),
        out_shape=jax.ShapeDtypeStruct((b, s, D_MODEL), BF16),
        grid=(b, HEADS),
        in_specs=[
            pl.BlockSpec((1, HEAD_W, s), lambda i, j: (i, j, 0)),
            pl.BlockSpec((1, s, HEAD_W), lambda i, j: (i, 0, j)),
            pl.BlockSpec((1, HEAD_W, s), lambda i, j: (i, j, 0)),
            pl.BlockSpec((1, tq // tk + 1, tk, tq), lambda i, j: (j, 0, 0, 0)),
            _resident(lam_p.shape),
            _resident((HEAD_W, 1)),
        ],
        out_specs=pl.BlockSpec((1, s, HEAD_W), lambda i, j: (i, 0, j)),
        scratch_shapes=[pltpu.VMEM((2, tk, tq), F32), pltpu.VMEM((2, tk, tq), F32),
                        pltpu.VMEM((2, HEAD_W, tq), F32),
                        pltpu.VMEM((2, 1, tq), F32), pltpu.VMEM((2, 1, tq), F32)],
        compiler_params=pltpu.CompilerParams(
            dimension_semantics=("arbitrary", "arbitrary"), vmem_limit_bytes=VMEM_LIMIT),
        name="diff_attention",
    )(qt, kk3, vt, bias, lam_p, sw_col)


def _head_major(w):
    d_in = w.shape[0]
    return w.reshape(d_in, 2, HEADS, HALF_W).transpose(0, 2, 1, 3).reshape(d_in, D_MODEL)


def kernel(x, norm_w, ffn_w_in, ffn_w_out, hgrn_w_in, hgrn_lower_bounds, hgrn_gnorm_w, hgrn_w_out,
           kv_norm_w, w_kv, rel_bias, diff_w_q, diff_lambda, diff_subln_w, diff_w_out, final_norm_w):
    b, s, d = x.shape
    assert d == D_MODEL and s % ATT_Q == 0 and s % FFN_ROWS == 0 and s % HGRN_ROWS == 0
    assert ATT_Q % ATT_KV == 0 and ATT_KV >= MAX_DISTANCE
    n = b * s

    def ffn_w(l, j):
        w_in = ffn_w_in[l, j].astype(BF16)
        return (norm_w[l, 2 * j].reshape(1, d), w_in[:, :D_FF], w_in[:, D_FF:], ffn_w_out[l, j].astype(BF16))

    wkv = jnp.concatenate([_head_major(w_kv[:, :D_MODEL]), w_kv[:, D_MODEL:]], axis=1).astype(BF16)
    wq = _head_major(diff_w_q[0]).astype(BF16)
    lambda_init = 0.8 - 0.6 * math.exp(-0.3 * 1)

    bias = _bias_tiles(rel_bias)

    h = _ffn(x.reshape(n, d), *ffn_w(0, 0), batch=b)
    h = _hgrn(h.reshape(b, s, d), norm_w[0, 1].reshape(1, d), hgrn_w_in[0].astype(BF16),
              hgrn_lower_bounds, hgrn_gnorm_w[0].reshape(1, HEAD_W), hgrn_w_out[0].astype(BF16))
    h, kk, vt = _ffn(h.reshape(n, d), *ffn_w(0, 1), batch=b, epilogue="kv",
                     epi_args=(kv_norm_w.reshape(1, d), wkv))
    h, qt = _ffn(h, *ffn_w(1, 0), batch=b, epilogue="q", epi_args=(norm_w[1, 1].reshape(1, d), wq))
    o = _attention(qt, kk.reshape(b, s, d), vt, bias, diff_lambda[0],
                   diff_subln_w[0].reshape(HEAD_W, 1), lambda_init=lambda_init)
    out = _ffn(h, *ffn_w(1, 1), batch=b, prologue="attn_out",
               pro_args=(o.reshape(n, d), diff_w_out[0].astype(BF16)),
               epilogue="final", epi_args=(final_norm_w.reshape(1, d),))
    return out.reshape(b, s, d)
```

```python
import functools
import math

import numpy as np
import jax
import jax.numpy as jnp
from jax import lax
from jax.experimental import pallas as pl
from jax.experimental.pallas import tpu as pltpu

F32 = jnp.float32
BF16 = jnp.bfloat16

D_MODEL = 1024
D_FF = 2816
EPS = 1e-6
FFN_RES = 0.5
HEADS = 8
HEAD_W = D_MODEL // HEADS
HALF_W = HEAD_W // 2
HGRN_CHUNK = 64
N_BUCKETS = 32
MAX_DISTANCE = 128
LOG2E = math.log2(math.e)
NEG = -1e30

FFN_ROWS = 512
FFN_COLS = 256
HGRN_ROWS = 256
ATT_TILE = 512
V_ROWS = HEAD_W + 16
VMEM_LIMIT = 56 * 1024 * 1024

NT_DIMS = (((1,), (1,)), ((), ()))
TN_DIMS = (((0,), (0,)), ((), ()))


def _dot(a, b):
    return jnp.dot(a, b, preferred_element_type=F32)


def _rms(x, w):
    return x * lax.rsqrt(jnp.mean(x * x, axis=-1, keepdims=True) + EPS) * w


def _silu(x):
    return x * jax.nn.sigmoid(x)


def _resident(shape):
    zeros = (0,) * len(shape)
    return pl.BlockSpec(shape, lambda *_: zeros, pipeline_mode=pl.Buffered(1))


def _ffn_kernel(*refs, prologue, epilogue):
    it = iter(refs)
    h_ref = next(it)
    if prologue == "attn_out":
        o_ref, wo_ref = next(it), next(it)
    nw_ref, wg_ref, wu_ref, wd_ref = next(it), next(it), next(it), next(it)
    if epilogue is not None:
        enw_ref = next(it)
    if epilogue in ("kv", "q"):
        ew_ref = next(it)
    outs = [next(it) for _ in range({None: 1, "final": 1, "q": 2, "kv": 3}[epilogue])]
    act_ref = next(it)

    h = h_ref[...]
    if prologue == "attn_out":
        h = h + _dot(o_ref[...], wo_ref[...])
    xn = _rms(h, nw_ref[...]).astype(BF16)
    for c in range(D_FF // FFN_COLS):
        sl = slice(c * FFN_COLS, (c + 1) * FFN_COLS)
        g = _dot(xn, wg_ref[:, sl])
        u = _dot(xn, wu_ref[:, sl])
        act_ref[:, sl] = (_silu(g) * u).astype(BF16)
    h = h + FFN_RES * _dot(act_ref[...], wd_ref[...])

    if epilogue == "final":
        outs[0][...] = _rms(h, enw_ref[...])
        return
    outs[0][...] = h
    if epilogue == "kv":
        kv = _dot(_rms(h, enw_ref[...]).astype(BF16), ew_ref[...])
        outs[1][...] = kv[:, :D_MODEL].astype(BF16)
        vt = kv[:, D_MODEL:].T.astype(BF16)
        for hh in range(HEADS):
            outs[2][0, hh * V_ROWS:hh * V_ROWS + HEAD_W, :] = vt[hh * HEAD_W:(hh + 1) * HEAD_W]
            outs[2][0, hh * V_ROWS + HEAD_W:(hh + 1) * V_ROWS, :] = jnp.ones((V_ROWS - HEAD_W, vt.shape[1]), BF16)
    elif epilogue == "q":
        q = _dot(_rms(h, enw_ref[...]).astype(BF16), ew_ref[...])
        outs[1][0] = (q * (HALF_W ** -0.5 * LOG2E)).T.astype(BF16)


def _ffn(h, nw, wg, wu, wd, *, batch, prologue=None, pro_args=(), epilogue=None, epi_args=()):
    n = h.shape[0]
    seq = n // batch
    tm = FFN_ROWS
    per_seq = seq // tm
    row = pl.BlockSpec((tm, D_MODEL), lambda i: (i, 0))
    tr = pl.BlockSpec((1, D_MODEL, tm), lambda i: (i // per_seq, 0, i % per_seq))
    in_specs, args = [row], [h]
    if prologue == "attn_out":
        in_specs += [row, _resident((D_MODEL, D_MODEL))]
        args += list(pro_args)
    in_specs += [_resident((1, D_MODEL)), _resident((D_MODEL, D_FF)), _resident((D_MODEL, D_FF)),
                 _resident((D_FF, D_MODEL))]
    args += [nw, wg, wu, wd]
    if epilogue is not None:
        in_specs.append(_resident((1, D_MODEL)))
    if epilogue in ("kv", "q"):
        in_specs.append(_resident(epi_args[1].shape))
    args += list(epi_args)
    h_shape = jax.ShapeDtypeStruct((n, D_MODEL), F32)
    t_shape = jax.ShapeDtypeStruct((batch, D_MODEL, seq), BF16)
    if epilogue in (None, "final"):
        out_shape, out_specs = h_shape, row
    elif epilogue == "q":
        out_shape, out_specs = (h_shape, t_shape), (row, tr)
    else:
        out_shape = (h_shape, jax.ShapeDtypeStruct((n, D_MODEL), BF16),
                     jax.ShapeDtypeStruct((batch, HEADS * V_ROWS, seq), BF16))
        out_specs = (row, row, pl.BlockSpec((1, HEADS * V_ROWS, tm), tr.index_map))
    return pl.pallas_call(
        functools.partial(_ffn_kernel, prologue=prologue, epilogue=epilogue),
        out_shape=out_shape,
        grid=(n // tm,),
        in_specs=in_specs,
        out_specs=out_specs,
        scratch_shapes=[pltpu.VMEM((tm, D_FF), BF16)],
        compiler_params=pltpu.CompilerParams(
            dimension_semantics=("arbitrary",), vmem_limit_bytes=VMEM_LIMIT),
        name="ffn_" + str(prologue) + "_" + str(epilogue),
    )(*args)


def _hgrn_kernel(h_ref, nw_ref, win_ref, lbraw_ref, ltri_ref, gw_ref, wout_ref, out_ref,
                 st_ref, o_ref, og_ref):
    t = h_ref.shape[1]
    c = HGRN_CHUNK

    @pl.when(pl.program_id(1) == 0)
    def _():
        st_ref[...] = jnp.zeros_like(st_ref)

    h = h_ref[0]
    z = _dot(_rms(h, nw_ref[...]).astype(BF16), win_ref[...])
    q = _silu(z[:, :D_MODEL])
    lbr = lbraw_ref[...]
    e = jnp.exp(lbr - jnp.max(lbr, axis=0, keepdims=True))
    lb = e[0:1] / jnp.sum(e, axis=0, keepdims=True)
    f = lb + (1.0 - lb) * jax.nn.sigmoid(z[:, D_MODEL:2 * D_MODEL])
    k = 1.0 - f
    logf = jnp.log(f)
    hi = logf.astype(BF16)
    lo = (logf - hi.astype(F32)).astype(BF16)
    ltri = ltri_ref[...]
    bc = _dot(ltri, hi) + _dot(ltri, lo)
    v = z[:, 2 * D_MODEL:3 * D_MODEL]

    tril = lax.broadcasted_iota(jnp.int32, (c, c), 0) >= lax.broadcasted_iota(jnp.int32, (c, c), 1)
    for n in range(t // c):
        r = slice(n * c, (n + 1) * c)
        bcn = bc[r]
        b_last = bcn[c - 1:c]
        b_mid = bcn[c // 2 - 1:c // 2]
        qn, kn = q[r], k[r]
        qd = (qn * jnp.exp(bcn)).astype(BF16)
        kd = (kn * jnp.exp(b_last - bcn)).astype(BF16)
        qi = (qn * jnp.exp(bcn - b_mid)).astype(BF16)
        ki = (kn * jnp.exp(b_mid - bcn)).astype(BF16)
        dec = jnp.exp(b_last)
        vb = v[r].astype(BF16)
        for hh in range(HEADS):
            cs = slice(hh * HEAD_W, (hh + 1) * HEAD_W)
            a = lax.dot_general(qi[:, cs], ki[:, cs], NT_DIMS, preferred_element_type=F32)
            a = jnp.where(tril, a, 0.0).astype(BF16)
            st = st_ref[hh]
            o_ref[r, cs] = _dot(a, vb[:, cs]) + lax.dot_general(
                qd[:, cs], st.astype(BF16), NT_DIMS, preferred_element_type=F32)
            ut = lax.dot_general(vb[:, cs], kd[:, cs], TN_DIMS, preferred_element_type=F32)
            st_ref[hh] = st * dec[:, cs] + ut

    gate = _silu(z[:, 3 * D_MODEL:])
    gw = gw_ref[...]
    for hh in range(HEADS):
        cs = slice(hh * HEAD_W, (hh + 1) * HEAD_W)
        og_ref[:, cs] = (_rms(o_ref[:, cs], gw) * gate[:, cs]).astype(BF16)
    out_ref[0] = h + _dot(og_ref[...], wout_ref[...])


def _hgrn(h3, nw, win, lbraw, gw, wout):
    b, s, _ = h3.shape
    t = HGRN_ROWS
    idx = np.arange(t)
    ltri = jnp.asarray((idx[:, None] // HGRN_CHUNK == idx[None, :] // HGRN_CHUNK)
                       & (idx[None, :] <= idx[:, None]), BF16)
    blk = pl.BlockSpec((1, t, D_MODEL), lambda i, j: (i, j, 0))
    return pl.pallas_call(
        _hgrn_kernel,
        out_shape=jax.ShapeDtypeStruct(h3.shape, F32),
        grid=(b, s // t),
        in_specs=[blk, _resident((1, D_MODEL)), _resident(win.shape), _resident(lbraw.shape),
                  _resident((t, t)), _resident((1, HEAD_W)), _resident((D_MODEL, D_MODEL))],
        out_specs=blk,
        scratch_shapes=[pltpu.VMEM((HEADS, HEAD_W, HEAD_W), F32),
                        pltpu.VMEM((t, D_MODEL), F32),
                        pltpu.VMEM((t, D_MODEL), BF16)],
        compiler_params=pltpu.CompilerParams(
            dimension_semantics=("arbitrary", "arbitrary"), vmem_limit_bytes=VMEM_LIMIT),
        name="hgrn_layer",
    )(h3, nw, win, lbraw, ltri, gw, wout)


def _bucket_starts():
    n = np.arange(4 * MAX_DISTANCE, dtype=np.int64)
    max_exact = N_BUCKETS // 2
    nf = np.maximum(n, 1).astype(np.float32)
    large = max_exact + (np.log(nf / np.float32(max_exact)) / np.float32(math.log(MAX_DISTANCE / max_exact))
                         * np.float32(N_BUCKETS - max_exact)).astype(np.int32)
    bucket = np.where(n < max_exact, n, np.minimum(large, N_BUCKETS - 1))
    assert np.all(np.diff(bucket) >= 0) and bucket[-1] == N_BUCKETS - 1
    return [int(np.argmax(bucket >= b)) for b in range(N_BUCKETS)]


def _bias_kernel(tab_ref, out_ref, *, starts):
    hh, w = pl.program_id(0), pl.program_id(1)
    t = out_ref.shape[2]
    d = (lax.broadcasted_iota(jnp.int32, (t, t), 1) - lax.broadcasted_iota(jnp.int32, (t, t), 0)) + w * t
    val = jnp.full((t, t), tab_ref[0, hh], F32)
    for b in range(1, N_BUCKETS):
        val = jnp.where(d >= starts[b], tab_ref[b, hh], val)
    out_ref[0, 0] = jnp.where(d >= 0, (val - tab_ref[N_BUCKETS - 1, hh]) * LOG2E, NEG)


def _bias_tiles(rel_bias):
    t = ATT_TILE
    return pl.pallas_call(
        functools.partial(_bias_kernel, starts=_bucket_starts()),
        out_shape=jax.ShapeDtypeStruct((HEADS, 2, t, t), F32),
        grid=(HEADS, 2),
        in_specs=[pl.BlockSpec(memory_space=pltpu.SMEM)],
        out_specs=pl.BlockSpec((1, 1, t, t), lambda i, j: (i, j, 0, 0)),
        compiler_params=pltpu.CompilerParams(dimension_semantics=("arbitrary", "arbitrary")),
        name="t5_bias_tiles",
    )(rel_bias)


def _attn_kernel(qt_ref, kk_ref, vt_ref, bias_ref, lam_ref, sw_ref, o_ref,
                 qa_ref, qb_ref, sa_ref, sb_ref, acc_ref, m_ref, *, lambda_init):
    t = ATT_TILE
    nq = kk_ref.shape[1] // t
    n_far = (nq - 1) * (nq - 2) // 2
    n_near = 2 * nq - 1
    unroll = 2

    row = lax.broadcasted_iota(jnp.int32, qt_ref.shape[1:], 0)
    qt = qt_ref[0]
    qa_ref[...] = jnp.where(row < HALF_W, qt, jnp.zeros_like(qt))
    qb_ref[...] = jnp.where(row >= HALF_W, qt, jnp.zeros_like(qt))
    acc_ref[...] = jnp.zeros_like(acc_ref)
    m_ref[...] = jnp.full_like(m_ref, NEG)

    def scores(qi, ki, dst):
        cols = pl.ds(pl.multiple_of(qi * t, t), t)
        kt = kk_ref[0, pl.ds(pl.multiple_of(ki * t, t), t), :]
        dst[0] = _dot(kt, qa_ref[:, cols])
        dst[1] = _dot(kt, qb_ref[:, cols])

    def accumulate(qi, ki, src, near):
        vt = vt_ref[0, :, pl.ds(pl.multiple_of(ki * t, t), t)]
        for mp in range(2):
            s = src[mp]
            if near is not None:
                s = s + bias_ref[0, near]
            m_old = m_ref[qi, mp]
            m_new = jnp.maximum(m_old, jnp.max(s, axis=0, keepdims=True))
            p = jnp.exp2(s - m_new).astype(BF16)
            acc_ref[qi, mp] = jnp.exp2(m_old - m_new) * acc_ref[qi, mp] + _dot(vt, p)
            m_ref[qi, mp] = m_new

    def far_next(qi, ki):
        wrap = ki >= qi - 2
        return jnp.where(wrap, qi + 1, qi), jnp.where(wrap, 0, ki + 1)

    def near_next(qi, near):
        return jnp.where(near == 0, qi + 1, qi), 1 - near

    def run(n_steps, first_buf, body, carry):
        bufs = (sa_ref, sb_ref) if first_buf == 0 else (sb_ref, sa_ref)

        def group(_, c):
            for u in range(unroll):
                c = body(c, bufs[u % 2], bufs[(u + 1) % 2])
            return c

        carry = lax.fori_loop(0, n_steps // unroll, group, carry)
        for u in range(n_steps % unroll):
            carry = body(carry, bufs[u % 2], bufs[(u + 1) % 2])
        return carry

    def far_step(c, src, dst):
        qi, ki = c
        qn, kn = far_next(qi, ki)
        last = (qi == nq - 1) & (ki == nq - 3)
        scores(jnp.where(last, 0, qn), jnp.where(last, 0, kn), dst)
        accumulate(qi, ki, src, None)
        return qn, kn

    def near_step(c, src, dst):
        qi, near = c
        qn, nn = near_next(qi, near)
        qn = jnp.minimum(qn, nq - 1)
        scores(qn, qn - nn, dst)
        accumulate(qi, qi - near, src, near)
        return qn, nn

    if n_far > 0:
        scores(2, 0, sa_ref)
        run(n_far, 0, far_step, (jnp.int32(2), jnp.int32(0)))
    else:
        scores(0, 0, sa_ref)
    run(n_near, n_far % 2, near_step, (jnp.int32(0), jnp.int32(0)))

    lp = lam_ref[...]
    lam = (jnp.exp(jnp.sum(lp[0:1] * lp[1:2], axis=1, keepdims=True))
           - jnp.exp(jnp.sum(lp[2:3] * lp[3:4], axis=1, keepdims=True)) + lambda_init)

    def finalize(qi, carry):
        a1, a2 = acc_ref[qi, 0], acc_ref[qi, 1]
        ot = (a1[:HEAD_W] * (1.0 / a1[HEAD_W:HEAD_W + 1])
              - lam * (a2[:HEAD_W] * (1.0 / a2[HEAD_W:HEAD_W + 1])))
        ms = jnp.mean(ot * ot, axis=0, keepdims=True)
        y = ot * lax.rsqrt(ms + EPS) * sw_ref[...] * (1.0 - lambda_init)
        o_ref[0, pl.ds(pl.multiple_of(qi * t, t), t), :] = y.T.astype(BF16)
        return carry

    lax.fori_loop(0, nq, finalize, 0)


def _attention(qt, kk3, vt, bias, lam_p, sw_col, *, lambda_init):
    b, s, _ = kk3.shape
    t = ATT_TILE
    nq = s // t
    return pl.pallas_call(
        functools.partial(_attn_kernel, lambda_init=lambda_init),
        out_shape=jax.ShapeDtypeStruct((b, s, D_MODEL), BF16),
        grid=(b, HEADS),
        in_specs=[
            pl.BlockSpec((1, HEAD_W, s), lambda i, j: (i, j, 0)),
            pl.BlockSpec((1, s, HEAD_W), lambda i, j: (i, 0, j)),
            pl.BlockSpec((1, V_ROWS, s), lambda i, j: (i, j, 0)),
            pl.BlockSpec((1, 2, t, t), lambda i, j: (j, 0, 0, 0)),
            _resident(lam_p.shape),
            _resident((HEAD_W, 1)),
        ],
        out_specs=pl.BlockSpec((1, s, HEAD_W), lambda i, j: (i, 0, j)),
        scratch_shapes=[pltpu.VMEM((HEAD_W, s), BF16), pltpu.VMEM((HEAD_W, s), BF16),
                        pltpu.VMEM((2, t, t), F32), pltpu.VMEM((2, t, t), F32),
                        pltpu.VMEM((nq, 2, V_ROWS, t), F32),
                        pltpu.VMEM((nq, 2, 1, t), F32)],
        compiler_params=pltpu.CompilerParams(
            dimension_semantics=("arbitrary", "arbitrary"), vmem_limit_bytes=VMEM_LIMIT),
        name="diff_attention",
    )(qt, kk3, vt, bias, lam_p, sw_col)


def _head_major(w):
    d_in = w.shape[0]
    return w.reshape(d_in, 2, HEADS, HALF_W).transpose(0, 2, 1, 3).reshape(d_in, D_MODEL)


def kernel(x, norm_w, ffn_w_in, ffn_w_out, hgrn_w_in, hgrn_lower_bounds, hgrn_gnorm_w, hgrn_w_out,
           kv_norm_w, w_kv, rel_bias, diff_w_q, diff_lambda, diff_subln_w, diff_w_out, final_norm_w):
    b, s, d = x.shape
    assert d == D_MODEL and s % ATT_TILE == 0 and s % FFN_ROWS == 0 and s % HGRN_ROWS == 0
    assert ATT_TILE >= MAX_DISTANCE
    n = b * s

    def ffn_w(l, j):
        w_in = ffn_w_in[l, j].astype(BF16)
        return (norm_w[l, 2 * j].reshape(1, d), w_in[:, :D_FF], w_in[:, D_FF:], ffn_w_out[l, j].astype(BF16))

    wkv = jnp.concatenate([_head_major(w_kv[:, :D_MODEL]), w_kv[:, D_MODEL:]], axis=1).astype(BF16)
    wq = _head_major(diff_w_q[0]).astype(BF16)
    lambda_init = 0.8 - 0.6 * math.exp(-0.3 * 1)

    bias = _bias_tiles(rel_bias)

    h = _ffn(x.reshape(n, d), *ffn_w(0, 0), batch=b)
    h = _hgrn(h.reshape(b, s, d), norm_w[0, 1].reshape(1, d), hgrn_w_in[0].astype(BF16),
              hgrn_lower_bounds, hgrn_gnorm_w[0].reshape(1, HEAD_W), hgrn_w_out[0].astype(BF16))
    h, kk, vt = _ffn(h.reshape(n, d), *ffn_w(0, 1), batch=b, epilogue="kv",
                     epi_args=(kv_norm_w.reshape(1, d), wkv))
    h, qt = _ffn(h, *ffn_w(1, 0), batch=b, epilogue="q", epi_args=(norm_w[1, 1].reshape(1, d), wq))
    o = _attention(qt, kk.reshape(b, s, d), vt, bias, diff_lambda[0],
                   diff_subln_w[0].reshape(HEAD_W, 1), lambda_init=lambda_init)
    out = _ffn(h, *ffn_w(1, 1), batch=b, prologue="attn_out",
               pro_args=(o.reshape(n, d), diff_w_out[0].astype(BF16)),
               epilogue="final", epi_args=(final_norm_w.reshape(1, d),))
    return out.reshape(b, s, d)
```

```python
import functools
import math

import numpy as np
import jax
import jax.numpy as jnp
from jax import lax
from jax.experimental import pallas as pl
from jax.experimental.pallas import tpu as pltpu

F32 = jnp.float32
BF16 = jnp.bfloat16

D_MODEL = 1024
D_FF = 2816
EPS = 1e-6
FFN_RES = 0.5
HEADS = 8
HEAD_W = D_MODEL // HEADS
HALF_W = HEAD_W // 2
HGRN_CHUNK = 64
N_BUCKETS = 32
MAX_DISTANCE = 128
LOG2E = math.log2(math.e)
NEG = -1e30

FFN_ROWS = 512
FFN_COLS = 256
HGRN_ROWS = 256
ATT_TILE = 512
V_ROWS = HEAD_W + 16
VMEM_LIMIT = 56 * 1024 * 1024

NT_DIMS = (((1,), (1,)), ((), ()))
TN_DIMS = (((0,), (0,)), ((), ()))


def _dot(a, b):
    return jnp.dot(a, b, preferred_element_type=F32)


def _rms(x, w):
    return x * lax.rsqrt(jnp.mean(x * x, axis=-1, keepdims=True) + EPS) * w


def _silu(x):
    return x * jax.nn.sigmoid(x)


def _resident(shape):
    zeros = (0,) * len(shape)
    return pl.BlockSpec(shape, lambda *_: zeros, pipeline_mode=pl.Buffered(1))


def _ffn_kernel(*refs, prologue, epilogue):
    it = iter(refs)
    h_ref = next(it)
    if prologue == "attn_out":
        o_ref, wo_ref = next(it), next(it)
    nw_ref, wg_ref, wu_ref, wd_ref = next(it), next(it), next(it), next(it)
    if epilogue is not None:
        enw_ref = next(it)
    if epilogue in ("kv", "q"):
        ew_ref = next(it)
    outs = [next(it) for _ in range({None: 1, "final": 1, "q": 2, "kv": 3}[epilogue])]
    act_ref = next(it)

    h = h_ref[...]
    if prologue == "attn_out":
        h = h + _dot(o_ref[...], wo_ref[...])
    xn = _rms(h, nw_ref[...]).astype(BF16)
    for c in range(D_FF // FFN_COLS):
        sl = slice(c * FFN_COLS, (c + 1) * FFN_COLS)
        g = _dot(xn, wg_ref[:, sl])
        u = _dot(xn, wu_ref[:, sl])
        act_ref[:, sl] = (_silu(g) * u).astype(BF16)
    h = h + FFN_RES * _dot(act_ref[...], wd_ref[...])

    if epilogue == "final":
        outs[0][...] = _rms(h, enw_ref[...])
        return
    outs[0][...] = h
    if epilogue == "kv":
        kv = _dot(_rms(h, enw_ref[...]).astype(BF16), ew_ref[...])
        outs[1][...] = kv[:, :D_MODEL].astype(BF16)
        vt = kv[:, D_MODEL:].T.astype(BF16)
        for hh in range(HEADS):
            outs[2][0, hh * V_ROWS:hh * V_ROWS + HEAD_W, :] = vt[hh * HEAD_W:(hh + 1) * HEAD_W]
            outs[2][0, hh * V_ROWS + HEAD_W:(hh + 1) * V_ROWS, :] = jnp.ones((V_ROWS - HEAD_W, vt.shape[1]), BF16)
    elif epilogue == "q":
        q = _dot(_rms(h, enw_ref[...]).astype(BF16), ew_ref[...])
        outs[1][0] = (q * (HALF_W ** -0.5 * LOG2E)).T.astype(BF16)


def _ffn(h, nw, wg, wu, wd, *, batch, prologue=None, pro_args=(), epilogue=None, epi_args=()):
    n = h.shape[0]
    seq = n // batch
    tm = FFN_ROWS
    per_seq = seq // tm
    row = pl.BlockSpec((tm, D_MODEL), lambda i: (i, 0))
    tr = pl.BlockSpec((1, D_MODEL, tm), lambda i: (i // per_seq, 0, i % per_seq))
    in_specs, args = [row], [h]
    if prologue == "attn_out":
        in_specs += [row, _resident((D_MODEL, D_MODEL))]
        args += list(pro_args)
    in_specs += [_resident((1, D_MODEL)), _resident((D_MODEL, D_FF)), _resident((D_MODEL, D_FF)),
                 _resident((D_FF, D_MODEL))]
    args += [nw, wg, wu, wd]
    if epilogue is not None:
        in_specs.append(_resident((1, D_MODEL)))
    if epilogue in ("kv", "q"):
        in_specs.append(_resident(epi_args[1].shape))
    args += list(epi_args)
    h_shape = jax.ShapeDtypeStruct((n, D_MODEL), F32)
    t_shape = jax.ShapeDtypeStruct((batch, D_MODEL, seq), BF16)
    if epilogue in (None, "final"):
        out_shape, out_specs = h_shape, row
    elif epilogue == "q":
        out_shape, out_specs = (h_shape, t_shape), (row, tr)
    else:
        out_shape = (h_shape, jax.ShapeDtypeStruct((n, D_MODEL), BF16),
                     jax.ShapeDtypeStruct((batch, HEADS * V_ROWS, seq), BF16))
        out_specs = (row, row, pl.BlockSpec((1, HEADS * V_ROWS, tm), tr.index_map))
    return pl.pallas_call(
        functools.partial(_ffn_kernel, prologue=prologue, epilogue=epilogue),
        out_shape=out_shape,
        grid=(n // tm,),
        in_specs=in_specs,
        out_specs=out_specs,
        scratch_shapes=[pltpu.VMEM((tm, D_FF), BF16)],
        compiler_params=pltpu.CompilerParams(
            dimension_semantics=("arbitrary",), vmem_limit_bytes=VMEM_LIMIT),
        name="ffn_" + str(prologue) + "_" + str(epilogue),
    )(*args)


def _hgrn_kernel(h_ref, nw_ref, win_ref, lbraw_ref, ltri_ref, gw_ref, wout_ref, out_ref,
                 st_ref, o_ref, og_ref):
    t = h_ref.shape[1]
    c = HGRN_CHUNK

    @pl.when(pl.program_id(1) == 0)
    def _():
        st_ref[...] = jnp.zeros_like(st_ref)

    h = h_ref[0]
    z = _dot(_rms(h, nw_ref[...]).astype(BF16), win_ref[...])
    q = _silu(z[:, :D_MODEL])
    lbr = lbraw_ref[...]
    e = jnp.exp(lbr - jnp.max(lbr, axis=0, keepdims=True))
    lb = e[0:1] / jnp.sum(e, axis=0, keepdims=True)
    f = lb + (1.0 - lb) * jax.nn.sigmoid(z[:, D_MODEL:2 * D_MODEL])
    k = 1.0 - f
    logf = jnp.log(f)
    hi = logf.astype(BF16)
    lo = (logf - hi.astype(F32)).astype(BF16)
    ltri = ltri_ref[...]
    bc = _dot(ltri, hi) + _dot(ltri, lo)
    v = z[:, 2 * D_MODEL:3 * D_MODEL]

    tril = lax.broadcasted_iota(jnp.int32, (c, c), 0) >= lax.broadcasted_iota(jnp.int32, (c, c), 1)
    for n in range(t // c):
        r = slice(n * c, (n + 1) * c)
        bcn = bc[r]
        b_last = bcn[c - 1:c]
        b_mid = bcn[c // 2 - 1:c // 2]
        qn, kn = q[r], k[r]
        qd = (qn * jnp.exp(bcn)).astype(BF16)
        kd = (kn * jnp.exp(b_last - bcn)).astype(BF16)
        qi = (qn * jnp.exp(bcn - b_mid)).astype(BF16)
        ki = (kn * jnp.exp(b_mid - bcn)).astype(BF16)
        dec = jnp.exp(b_last)
        vb = v[r].astype(BF16)
        for hh in range(HEADS):
            cs = slice(hh * HEAD_W, (hh + 1) * HEAD_W)
            a = lax.dot_general(qi[:, cs], ki[:, cs], NT_DIMS, preferred_element_type=F32)
            a = jnp.where(tril, a, 0.0).astype(BF16)
            st = st_ref[hh]
            o_ref[r, cs] = _dot(a, vb[:, cs]) + lax.dot_general(
                qd[:, cs], st.astype(BF16), NT_DIMS, preferred_element_type=F32)
            ut = lax.dot_general(vb[:, cs], kd[:, cs], TN_DIMS, preferred_element_type=F32)
            st_ref[hh] = st * dec[:, cs] + ut

    gate = _silu(z[:, 3 * D_MODEL:])
    gw = gw_ref[...]
    for hh in range(HEADS):
        cs = slice(hh * HEAD_W, (hh + 1) * HEAD_W)
        og_ref[:, cs] = (_rms(o_ref[:, cs], gw) * gate[:, cs]).astype(BF16)
    out_ref[0] = h + _dot(og_ref[...], wout_ref[...])


def _hgrn(h3, nw, win, lbraw, gw, wout):
    b, s, _ = h3.shape
    t = HGRN_ROWS
    idx = np.arange(t)
    ltri = jnp.asarray((idx[:, None] // HGRN_CHUNK == idx[None, :] // HGRN_CHUNK)
                       & (idx[None, :] <= idx[:, None]), BF16)
    blk = pl.BlockSpec((1, t, D_MODEL), lambda i, j: (i, j, 0))
    return pl.pallas_call(
        _hgrn_kernel,
        out_shape=jax.ShapeDtypeStruct(h3.shape, F32),
        grid=(b, s // t),
        in_specs=[blk, _resident((1, D_MODEL)), _resident(win.shape), _resident(lbraw.shape),
                  _resident((t, t)), _resident((1, HEAD_W)), _resident((D_MODEL, D_MODEL))],
        out_specs=blk,
        scratch_shapes=[pltpu.VMEM((HEADS, HEAD_W, HEAD_W), F32),
                        pltpu.VMEM((t, D_MODEL), F32),
                        pltpu.VMEM((t, D_MODEL), BF16)],
        compiler_params=pltpu.CompilerParams(
            dimension_semantics=("arbitrary", "arbitrary"), vmem_limit_bytes=VMEM_LIMIT),
        name="hgrn_layer",
    )(h3, nw, win, lbraw, ltri, gw, wout)


def _bucket_starts():
    n = np.arange(4 * MAX_DISTANCE, dtype=np.int64)
    max_exact = N_BUCKETS // 2
    nf = np.maximum(n, 1).astype(np.float32)
    large = max_exact + (np.log(nf / np.float32(max_exact)) / np.float32(math.log(MAX_DISTANCE / max_exact))
                         * np.float32(N_BUCKETS - max_exact)).astype(np.int32)
    bucket = np.where(n < max_exact, n, np.minimum(large, N_BUCKETS - 1))
    assert np.all(np.diff(bucket) >= 0) and bucket[-1] == N_BUCKETS - 1
    return [int(np.argmax(bucket >= b)) for b in range(N_BUCKETS)]


def _bias_kernel(tab_ref, out_ref, *, starts):
    hh, w = pl.program_id(0), pl.program_id(1)
    t = out_ref.shape[2]
    d = (lax.broadcasted_iota(jnp.int32, (t, t), 1) - lax.broadcasted_iota(jnp.int32, (t, t), 0)) + w * t
    val = jnp.full((t, t), tab_ref[0, hh], F32)
    for b in range(1, N_BUCKETS):
        val = jnp.where(d >= starts[b], tab_ref[b, hh], val)
    out_ref[0, 0] = jnp.where(d >= 0, (val - tab_ref[N_BUCKETS - 1, hh]) * LOG2E, NEG)


def _bias_tiles(rel_bias):
    t = ATT_TILE
    return pl.pallas_call(
        functools.partial(_bias_kernel, starts=_bucket_starts()),
        out_shape=jax.ShapeDtypeStruct((HEADS, 2, t, t), F32),
        grid=(HEADS, 2),
        in_specs=[pl.BlockSpec(memory_space=pltpu.SMEM)],
        out_specs=pl.BlockSpec((1, 1, t, t), lambda i, j: (i, j, 0, 0)),
        compiler_params=pltpu.CompilerParams(dimension_semantics=("arbitrary", "arbitrary")),
        name="t5_bias_tiles",
    )(rel_bias)


def _attn_kernel(qt_ref, kk_ref, vt_ref, bias_ref, lam_ref, sw_ref, o_ref,
                 qa_ref, qb_ref, sa_ref, sb_ref, acc_ref, m_ref, *, lambda_init):
    t = ATT_TILE
    nq = kk_ref.shape[1] // t
    n_far = (nq - 1) * (nq - 2) // 2
    n_near = 2 * nq - 1
    unroll = 4

    row = lax.broadcasted_iota(jnp.int32, qt_ref.shape[1:], 0)
    qt = qt_ref[0]
    qa_ref[...] = jnp.where(row < HALF_W, qt, jnp.zeros_like(qt))
    qb_ref[...] = jnp.where(row >= HALF_W, qt, jnp.zeros_like(qt))
    acc_ref[...] = jnp.zeros_like(acc_ref)
    m_ref[...] = jnp.full_like(m_ref, NEG)

    def scores(qi, ki, dst):
        cols = pl.ds(pl.multiple_of(qi * t, t), t)
        kt = kk_ref[0, pl.ds(pl.multiple_of(ki * t, t), t), :]
        dst[0] = _dot(kt, qa_ref[:, cols])
        dst[1] = _dot(kt, qb_ref[:, cols])

    def accumulate(qi, ki, src, near):
        vt = vt_ref[0, :, pl.ds(pl.multiple_of(ki * t, t), t)]
        for mp in range(2):
            s = src[mp]
            if near is not None:
                s = s + bias_ref[0, near]
            m_old = m_ref[qi, mp]
            m_new = jnp.maximum(m_old, jnp.max(s, axis=0, keepdims=True))
            p = jnp.exp2(s - m_new).astype(BF16)
            acc_ref[qi, mp] = jnp.exp2(m_old - m_new) * acc_ref[qi, mp] + _dot(vt, p)
            m_ref[qi, mp] = m_new

    def far_next(qi, ki):
        wrap = ki >= qi - 2
        return jnp.where(wrap, qi + 1, qi), jnp.where(wrap, 0, ki + 1)

    def near_next(qi, near):
        return jnp.where(near == 0, qi + 1, qi), 1 - near

    def run(n_steps, first_buf, body, carry):
        bufs = (sa_ref, sb_ref) if first_buf == 0 else (sb_ref, sa_ref)

        def group(_, c):
            for u in range(unroll):
                c = body(c, bufs[u % 2], bufs[(u + 1) % 2])
            return c

        carry = lax.fori_loop(0, n_steps // unroll, group, carry)
        for u in range(n_steps % unroll):
            carry = body(carry, bufs[u % 2], bufs[(u + 1) % 2])
        return carry

    def far_step(c, src, dst):
        qi, ki = c
        qn, kn = far_next(qi, ki)
        last = (qi == nq - 1) & (ki == nq - 3)
        scores(jnp.where(last, 0, qn), jnp.where(last, 0, kn), dst)
        accumulate(qi, ki, src, None)
        return qn, kn

    def near_step(c, src, dst):
        qi, near = c
        qn, nn = near_next(qi, near)
        qn = jnp.minimum(qn, nq - 1)
        scores(qn, qn - nn, dst)
        accumulate(qi, qi - near, src, near)
        return qn, nn

    if n_far > 0:
        scores(2, 0, sa_ref)
        run(n_far, 0, far_step, (jnp.int32(2), jnp.int32(0)))
    else:
        scores(0, 0, sa_ref)
    run(n_near, n_far % 2, near_step, (jnp.int32(0), jnp.int32(0)))

    lp = lam_ref[...]
    lam = (jnp.exp(jnp.sum(lp[0:1] * lp[1:2], axis=1, keepdims=True))
           - jnp.exp(jnp.sum(lp[2:3] * lp[3:4], axis=1, keepdims=True)) + lambda_init)

    def finalize(qi, carry):
        a1, a2 = acc_ref[qi, 0], acc_ref[qi, 1]
        ot = (a1[:HEAD_W] * (1.0 / a1[HEAD_W:HEAD_W + 1])
              - lam * (a2[:HEAD_W] * (1.0 / a2[HEAD_W:HEAD_W + 1])))
        ms = jnp.mean(ot * ot, axis=0, keepdims=True)
        y = ot * lax.rsqrt(ms + EPS) * sw_ref[...] * (1.0 - lambda_init)
        o_ref[0, pl.ds(pl.multiple_of(qi * t, t), t), :] = y.T.astype(BF16)
        return carry

    lax.fori_loop(0, nq, finalize, 0)


def _attention(qt, kk3, vt, bias, lam_p, sw_col, *, lambda_init):
    b, s, _ = kk3.shape
    t = ATT_TILE
    nq = s // t
    return pl.pallas_call(
        functools.partial(_attn_kernel, lambda_init=lambda_init),
        out_shape=jax.ShapeDtypeStruct((b, s, D_MODEL), BF16),
        grid=(b, HEADS),
        in_specs=[
            pl.BlockSpec((1, HEAD_W, s), lambda i, j: (i, j, 0)),
            pl.BlockSpec((1, s, HEAD_W), lambda i, j: (i, 0, j)),
            pl.BlockSpec((1, V_ROWS, s), lambda i, j: (i, j, 0)),
            pl.BlockSpec((1, 2, t, t), lambda i, j: (j, 0, 0, 0)),
            _resident(lam_p.shape),
            _resident((HEAD_W, 1)),
        ],
        out_specs=pl.BlockSpec((1, s, HEAD_W), lambda i, j: (i, 0, j)),
        scratch_shapes=[pltpu.VMEM((HEAD_W, s), BF16), pltpu.VMEM((HEAD_W, s), BF16),
                        pltpu.VMEM((2, t, t), F32), pltpu.VMEM((2, t, t), F32),
                        pltpu.VMEM((nq, 2, V_ROWS, t), F32),
                        pltpu.VMEM((nq, 2, 1, t), F32)],
        compiler_params=pltpu.CompilerParams(
            dimension_semantics=("arbitrary", "arbitrary"), vmem_limit_bytes=VMEM_LIMIT),
        name="diff_attention",
    )(qt, kk3, vt, bias, lam_p, sw_col)


def _head_major(w):
    d_in = w.shape[0]
    return w.reshape(d_in, 2, HEADS, HALF_W).transpose(0, 2, 1, 3).reshape(d_in, D_MODEL)


def kernel(x, norm_w, ffn_w_in, ffn_w_out, hgrn_w_in, hgrn_lower_bounds, hgrn_gnorm_w, hgrn_w_out,
           kv_norm_w, w_kv, rel_bias, diff_w_q, diff_lambda, diff_subln_w, diff_w_out, final_norm_w):
    b, s, d = x.shape
    assert d == D_MODEL and s % ATT_TILE == 0 and s % FFN_ROWS == 0 and s % HGRN_ROWS == 0
    assert ATT_TILE >= MAX_DISTANCE
    n = b * s

    def ffn_w(l, j):
        w_in = ffn_w_in[l, j].astype(BF16)
        return (norm_w[l, 2 * j].reshape(1, d), w_in[:, :D_FF], w_in[:, D_FF:], ffn_w_out[l, j].astype(BF16))

    wkv = jnp.concatenate([_head_major(w_kv[:, :D_MODEL]), w_kv[:, D_MODEL:]], axis=1).astype(BF16)
    wq = _head_major(diff_w_q[0]).astype(BF16)
    lambda_init = 0.8 - 0.6 * math.exp(-0.3 * 1)

    bias = _bias_tiles(rel_bias)

    h = _ffn(x.reshape(n, d), *ffn_w(0, 0), batch=b)
    h = _hgrn(h.reshape(b, s, d), norm_w[0, 1].reshape(1, d), hgrn_w_in[0].astype(BF16),
              hgrn_lower_bounds, hgrn_gnorm_w[0].reshape(1, HEAD_W), hgrn_w_out[0].astype(BF16))
    h, kk, vt = _ffn(h.reshape(n, d), *ffn_w(0, 1), batch=b, epilogue="kv",
                     epi_args=(kv_norm_w.reshape(1, d), wkv))
    h, qt = _ffn(h, *ffn_w(1, 0), batch=b, epilogue="q", epi_args=(norm_w[1, 1].reshape(1, d), wq))
    o = _attention(qt, kk.reshape(b, s, d), vt, bias, diff_lambda[0],
                   diff_subln_w[0].reshape(HEAD_W, 1), lambda_init=lambda_init)
    out = _ffn(h, *ffn_w(1, 1), batch=b, prologue="attn_out",
               pro_args=(o.reshape(n, d), diff_w_out[0].astype(BF16)),
               epilogue="final", epi_args=(final_norm_w.reshape(1, d),))
    return out.reshape(b, s, d)
```

```python
import functools
import math

import numpy as np
import jax
import jax.numpy as jnp
from jax import lax
from jax.experimental import pallas as pl
from jax.experimental.pallas import tpu as pltpu

F32 = jnp.float32
BF16 = jnp.bfloat16

D_MODEL = 1024
D_FF = 2816
EPS = 1e-6
FFN_RES = 0.5
HEADS = 8
HEAD_W = D_MODEL // HEADS
HALF_W = HEAD_W // 2
HGRN_CHUNK = 64
N_BUCKETS = 32
MAX_DISTANCE = 128
LOG2E = math.log2(math.e)
NEG = -1e30

FFN_ROWS = 512
FFN_COLS = 256
HGRN_ROWS = 256
HGRN_TILES = 4
ATT_TILE = 512
V_ROWS = HEAD_W + 16
VMEM_LIMIT = 56 * 1024 * 1024

NT_DIMS = (((1,), (1,)), ((), ()))
TN_DIMS = (((0,), (0,)), ((), ()))


def _dot(a, b):
    return jnp.dot(a, b, preferred_element_type=F32)


def _rms(x, w):
    return x * lax.rsqrt(jnp.mean(x * x, axis=-1, keepdims=True) + EPS) * w


def _silu(x):
    return x * jax.nn.sigmoid(x)


def _resident(shape):
    zeros = (0,) * len(shape)
    return pl.BlockSpec(shape, lambda *_: zeros, pipeline_mode=pl.Buffered(1))


def _ffn_kernel(*refs, prologue, epilogue):
    it = iter(refs)
    h_ref = next(it)
    if prologue == "attn_out":
        o_ref, wo_ref = next(it), next(it)
    nw_ref, wg_ref, wu_ref, wd_ref = next(it), next(it), next(it), next(it)
    if epilogue is not None:
        enw_ref = next(it)
    if epilogue in ("kv", "q"):
        ew_ref = next(it)
    outs = [next(it) for _ in range({None: 1, "final": 1, "q": 2, "kv": 3}[epilogue])]
    act_ref = next(it)

    h = h_ref[...]
    if prologue == "attn_out":
        h = h + _dot(o_ref[...], wo_ref[...])
    xn = _rms(h, nw_ref[...]).astype(BF16)
    for c in range(D_FF // FFN_COLS):
        sl = slice(c * FFN_COLS, (c + 1) * FFN_COLS)
        g = _dot(xn, wg_ref[:, sl])
        u = _dot(xn, wu_ref[:, sl])
        act_ref[:, sl] = (_silu(g) * u).astype(BF16)
    h = h + FFN_RES * _dot(act_ref[...], wd_ref[...])

    if epilogue == "final":
        outs[0][...] = _rms(h, enw_ref[...])
        return
    outs[0][...] = h
    if epilogue == "kv":
        kv = _dot(_rms(h, enw_ref[...]).astype(BF16), ew_ref[...])
        outs[1][...] = kv[:, :D_MODEL].astype(BF16)
        vt = kv[:, D_MODEL:].T.astype(BF16)
        for hh in range(HEADS):
            outs[2][0, hh * V_ROWS:hh * V_ROWS + HEAD_W, :] = vt[hh * HEAD_W:(hh + 1) * HEAD_W]
            outs[2][0, hh * V_ROWS + HEAD_W:(hh + 1) * V_ROWS, :] = jnp.ones((V_ROWS - HEAD_W, vt.shape[1]), BF16)
    elif epilogue == "q":
        q = _dot(_rms(h, enw_ref[...]).astype(BF16), ew_ref[...])
        outs[1][0] = (q * (HALF_W ** -0.5 * LOG2E)).T.astype(BF16)


def _ffn(h, nw, wg, wu, wd, *, batch, prologue=None, pro_args=(), epilogue=None, epi_args=()):
    n = h.shape[0]
    seq = n // batch
    tm = FFN_ROWS
    per_seq = seq // tm
    row = pl.BlockSpec((tm, D_MODEL), lambda i: (i, 0))
    tr = pl.BlockSpec((1, D_MODEL, tm), lambda i: (i // per_seq, 0, i % per_seq))
    in_specs, args = [row], [h]
    if prologue == "attn_out":
        in_specs += [row, _resident((D_MODEL, D_MODEL))]
        args += list(pro_args)
    in_specs += [_resident((1, D_MODEL)), _resident((D_MODEL, D_FF)), _resident((D_MODEL, D_FF)),
                 _resident((D_FF, D_MODEL))]
    args += [nw, wg, wu, wd]
    if epilogue is not None:
        in_specs.append(_resident((1, D_MODEL)))
    if epilogue in ("kv", "q"):
        in_specs.append(_resident(epi_args[1].shape))
    args += list(epi_args)
    h_shape = jax.ShapeDtypeStruct((n, D_MODEL), F32)
    t_shape = jax.ShapeDtypeStruct((batch, D_MODEL, seq), BF16)
    if epilogue in (None, "final"):
        out_shape, out_specs = h_shape, row
    elif epilogue == "q":
        out_shape, out_specs = (h_shape, t_shape), (row, tr)
    else:
        out_shape = (h_shape, jax.ShapeDtypeStruct((n, D_MODEL), BF16),
                     jax.ShapeDtypeStruct((batch, HEADS * V_ROWS, seq), BF16))
        out_specs = (row, row, pl.BlockSpec((1, HEADS * V_ROWS, tm), tr.index_map))
    return pl.pallas_call(
        functools.partial(_ffn_kernel, prologue=prologue, epilogue=epilogue),
        out_shape=out_shape,
        grid=(n // tm,),
        in_specs=in_specs,
        out_specs=out_specs,
        scratch_shapes=[pltpu.VMEM((tm, D_FF), BF16)],
        compiler_params=pltpu.CompilerParams(
            dimension_semantics=("arbitrary",), vmem_limit_bytes=VMEM_LIMIT),
        name="ffn_" + str(prologue) + "_" + str(epilogue),
    )(*args)


def _hgrn_kernel(h_ref, nw_ref, win_ref, lbraw_ref, ltri_ref, gw_ref, wout_ref, out_ref,
                 z_ref, st_ref, o_ref, og_ref):
    t = HGRN_ROWS
    n_buf = z_ref.shape[0]

    @pl.when(pl.program_id(1) == 0)
    def _():
        st_ref[...] = jnp.zeros_like(st_ref)

    def project(i):
        h = h_ref[0, i * t:(i + 1) * t]
        z_ref[i % n_buf] = _dot(_rms(h, nw_ref[...]).astype(BF16), win_ref[...])

    project(0)
    for i in range(HGRN_TILES):
        if i + 1 < HGRN_TILES:
            project(i + 1)
        out_ref[0, i * t:(i + 1) * t] = _hgrn_mix(
            h_ref[0, i * t:(i + 1) * t], z_ref.at[i % n_buf], o_ref.at[i % 2], og_ref.at[i % 2],
            lbraw_ref=lbraw_ref, ltri_ref=ltri_ref, gw_ref=gw_ref, wout_ref=wout_ref, st_ref=st_ref)


def _hgrn_mix(h, z_ref, o_ref, og_ref, *, lbraw_ref, ltri_ref, gw_ref, wout_ref, st_ref):
    t = h.shape[0]
    c = HGRN_CHUNK
    q = _silu(z_ref[:, :D_MODEL])
    lbr = lbraw_ref[...]
    e = jnp.exp(lbr - jnp.max(lbr, axis=0, keepdims=True))
    lb = e[0:1] / jnp.sum(e, axis=0, keepdims=True)
    f = lb + (1.0 - lb) * jax.nn.sigmoid(z_ref[:, D_MODEL:2 * D_MODEL])
    k = 1.0 - f
    logf = jnp.log(f)
    hi = logf.astype(BF16)
    lo = (logf - hi.astype(F32)).astype(BF16)
    ltri = ltri_ref[...]
    bc = _dot(ltri, hi) + _dot(ltri, lo)
    v = z_ref[:, 2 * D_MODEL:3 * D_MODEL]

    tril = lax.broadcasted_iota(jnp.int32, (c, c), 0) >= lax.broadcasted_iota(jnp.int32, (c, c), 1)
    for n in range(t // c):
        r = slice(n * c, (n + 1) * c)
        bcn = bc[r]
        b_last = bcn[c - 1:c]
        b_mid = bcn[c // 2 - 1:c // 2]
        qn, kn = q[r], k[r]
        qd = (qn * jnp.exp(bcn)).astype(BF16)
        kd = (kn * jnp.exp(b_last - bcn)).astype(BF16)
        qi = (qn * jnp.exp(bcn - b_mid)).astype(BF16)
        ki = (kn * jnp.exp(b_mid - bcn)).astype(BF16)
        dec = jnp.exp(b_last)
        vb = v[r].astype(BF16)
        for hh in range(HEADS):
            cs = slice(hh * HEAD_W, (hh + 1) * HEAD_W)
            a = lax.dot_general(qi[:, cs], ki[:, cs], NT_DIMS, preferred_element_type=F32)
            a = jnp.where(tril, a, 0.0).astype(BF16)
            st = st_ref[hh]
            o_ref[r, cs] = _dot(a, vb[:, cs]) + lax.dot_general(
                qd[:, cs], st.astype(BF16), NT_DIMS, preferred_element_type=F32)
            ut = lax.dot_general(vb[:, cs], kd[:, cs], TN_DIMS, preferred_element_type=F32)
            st_ref[hh] = st * dec[:, cs] + ut

    gate = _silu(z_ref[:, 3 * D_MODEL:])
    gw = gw_ref[...]
    for hh in range(HEADS):
        cs = slice(hh * HEAD_W, (hh + 1) * HEAD_W)
        og_ref[:, cs] = (_rms(o_ref[:, cs], gw) * gate[:, cs]).astype(BF16)
    return h + _dot(og_ref[...], wout_ref[...])


def _hgrn(h3, nw, win, lbraw, gw, wout):
    b, s, _ = h3.shape
    t = HGRN_ROWS
    rows = HGRN_TILES * t
    idx = np.arange(t)
    ltri = jnp.asarray((idx[:, None] // HGRN_CHUNK == idx[None, :] // HGRN_CHUNK)
                       & (idx[None, :] <= idx[:, None]), BF16)
    blk = pl.BlockSpec((1, rows, D_MODEL), lambda i, j: (i, j, 0))
    return pl.pallas_call(
        _hgrn_kernel,
        out_shape=jax.ShapeDtypeStruct(h3.shape, F32),
        grid=(b, s // rows),
        in_specs=[blk, _resident((1, D_MODEL)), _resident(win.shape), _resident(lbraw.shape),
                  _resident((t, t)), _resident((1, HEAD_W)), _resident((D_MODEL, D_MODEL))],
        out_specs=blk,
        scratch_shapes=[pltpu.VMEM((min(3, HGRN_TILES), t, 4 * D_MODEL), F32),
                        pltpu.VMEM((HEADS, HEAD_W, HEAD_W), F32),
                        pltpu.VMEM((2, t, D_MODEL), F32),
                        pltpu.VMEM((2, t, D_MODEL), BF16)],
        compiler_params=pltpu.CompilerParams(
            dimension_semantics=("arbitrary", "arbitrary"), vmem_limit_bytes=VMEM_LIMIT),
        name="hgrn_layer",
    )(h3, nw, win, lbraw, ltri, gw, wout)


def _bucket_starts():
    n = np.arange(4 * MAX_DISTANCE, dtype=np.int64)
    max_exact = N_BUCKETS // 2
    nf = np.maximum(n, 1).astype(np.float32)
    large = max_exact + (np.log(nf / np.float32(max_exact)) / np.float32(math.log(MAX_DISTANCE / max_exact))
                         * np.float32(N_BUCKETS - max_exact)).astype(np.int32)
    bucket = np.where(n < max_exact, n, np.minimum(large, N_BUCKETS - 1))
    assert np.all(np.diff(bucket) >= 0) and bucket[-1] == N_BUCKETS - 1
    return [int(np.argmax(bucket >= b)) for b in range(N_BUCKETS)]


def _bias_kernel(tab_ref, out_ref, *, starts):
    hh, w = pl.program_id(0), pl.program_id(1)
    t = out_ref.shape[2]
    d = (lax.broadcasted_iota(jnp.int32, (t, t), 1) - lax.broadcasted_iota(jnp.int32, (t, t), 0)) + w * t
    val = jnp.full((t, t), tab_ref[0, hh], F32)
    for b in range(1, N_BUCKETS):
        val = jnp.where(d >= starts[b], tab_ref[b, hh], val)
    out_ref[0, 0] = jnp.where(d >= 0, (val - tab_ref[N_BUCKETS - 1, hh]) * LOG2E, NEG)


def _bias_tiles(rel_bias):
    t = ATT_TILE
    return pl.pallas_call(
        functools.partial(_bias_kernel, starts=_bucket_starts()),
        out_shape=jax.ShapeDtypeStruct((HEADS, 2, t, t), F32),
        grid=(HEADS, 2),
        in_specs=[pl.BlockSpec(memory_space=pltpu.SMEM)],
        out_specs=pl.BlockSpec((1, 1, t, t), lambda i, j: (i, j, 0, 0)),
        compiler_params=pltpu.CompilerParams(dimension_semantics=("arbitrary", "arbitrary")),
        name="t5_bias_tiles",
    )(rel_bias)


def _attn_kernel(qt_ref, kk_ref, vt_ref, bias_ref, lam_ref, sw_ref, o_ref,
                 qa_ref, qb_ref, sa_ref, sb_ref, acc_ref, m_ref, *, lambda_init):
    t = ATT_TILE
    nq = kk_ref.shape[1] // t
    n_far = (nq - 1) * (nq - 2) // 2
    n_near = 2 * nq - 1
    unroll = 8

    row = lax.broadcasted_iota(jnp.int32, qt_ref.shape[1:], 0)
    qt = qt_ref[0]
    qa_ref[...] = jnp.where(row < HALF_W, qt, jnp.zeros_like(qt))
    qb_ref[...] = jnp.where(row >= HALF_W, qt, jnp.zeros_like(qt))
    acc_ref[...] = jnp.zeros_like(acc_ref)
    m_ref[...] = jnp.full_like(m_ref, NEG)

    def scores(qi, ki, dst):
        cols = pl.ds(pl.multiple_of(qi * t, t), t)
        kt = kk_ref[0, pl.ds(pl.multiple_of(ki * t, t), t), :]
        dst[0] = _dot(kt, qa_ref[:, cols])
        dst[1] = _dot(kt, qb_ref[:, cols])

    def accumulate(qi, ki, src, near):
        vt = vt_ref[0, :, pl.ds(pl.multiple_of(ki * t, t), t)]
        for mp in range(2):
            s = src[mp]
            if near is not None:
                s = s + bias_ref[0, near]
            m_old = m_ref[qi, mp]
            m_new = jnp.maximum(m_old, jnp.max(s, axis=0, keepdims=True))
            p = jnp.exp2(s - m_new).astype(BF16)
            acc_ref[qi, mp] = jnp.exp2(m_old - m_new) * acc_ref[qi, mp] + _dot(vt, p)
            m_ref[qi, mp] = m_new

    def far_next(qi, ki):
        wrap = ki >= qi - 2
        return jnp.where(wrap, qi + 1, qi), jnp.where(wrap, 0, ki + 1)

    def near_next(qi, near):
        return jnp.where(near == 0, qi + 1, qi), 1 - near

    def run(n_steps, first_buf, body, carry):
        bufs = (sa_ref, sb_ref) if first_buf == 0 else (sb_ref, sa_ref)

        def group(_, c):
            for u in range(unroll):
                c = body(c, bufs[u % 2], bufs[(u + 1) % 2])
            return c

        carry = lax.fori_loop(0, n_steps // unroll, group, carry)
        for u in range(n_steps % unroll):
            carry = body(carry, bufs[u % 2], bufs[(u + 1) % 2])
        return carry

    def far_step(c, src, dst):
        qi, ki = c
        qn, kn = far_next(qi, ki)
        last = (qi == nq - 1) & (ki == nq - 3)
        scores(jnp.where(last, 0, qn), jnp.where(last, 0, kn), dst)
        accumulate(qi, ki, src, None)
        return qn, kn

    def near_step(c, src, dst):
        qi, near = c
        qn, nn = near_next(qi, near)
        qn = jnp.minimum(qn, nq - 1)
        scores(qn, qn - nn, dst)
        accumulate(qi, qi - near, src, near)
        return qn, nn

    if n_far > 0:
        scores(2, 0, sa_ref)
        run(n_far, 0, far_step, (jnp.int32(2), jnp.int32(0)))
    else:
        scores(0, 0, sa_ref)
    run(n_near, n_far % 2, near_step, (jnp.int32(0), jnp.int32(0)))

    lp = lam_ref[...]
    lam = (jnp.exp(jnp.sum(lp[0:1] * lp[1:2], axis=1, keepdims=True))
           - jnp.exp(jnp.sum(lp[2:3] * lp[3:4], axis=1, keepdims=True)) + lambda_init)

    def finalize(qi, carry):
        a1, a2 = acc_ref[qi, 0], acc_ref[qi, 1]
        ot = (a1[:HEAD_W] * (1.0 / a1[HEAD_W:HEAD_W + 1])
              - lam * (a2[:HEAD_W] * (1.0 / a2[HEAD_W:HEAD_W + 1])))
        ms = jnp.mean(ot * ot, axis=0, keepdims=True)
        y = ot * lax.rsqrt(ms + EPS) * sw_ref[...] * (1.0 - lambda_init)
        o_ref[0, pl.ds(pl.multiple_of(qi * t, t), t), :] = y.T.astype(BF16)
        return carry

    lax.fori_loop(0, nq, finalize, 0)


def _attention(qt, kk3, vt, bias, lam_p, sw_col, *, lambda_init):
    b, s, _ = kk3.shape
    t = ATT_TILE
    nq = s // t
    return pl.pallas_call(
        functools.partial(_attn_kernel, lambda_init=lambda_init),
        out_shape=jax.ShapeDtypeStruct((b, s, D_MODEL), BF16),
        grid=(b, HEADS),
        in_specs=[
            pl.BlockSpec((1, HEAD_W, s), lambda i, j: (i, j, 0)),
            pl.BlockSpec((1, s, HEAD_W), lambda i, j: (i, 0, j)),
            pl.BlockSpec((1, V_ROWS, s), lambda i, j: (i, j, 0)),
            pl.BlockSpec((1, 2, t, t), lambda i, j: (j, 0, 0, 0)),
            _resident(lam_p.shape),
            _resident((HEAD_W, 1)),
        ],
        out_specs=pl.BlockSpec((1, s, HEAD_W), lambda i, j: (i, 0, j)),
        scratch_shapes=[pltpu.VMEM((HEAD_W, s), BF16), pltpu.VMEM((HEAD_W, s), BF16),
                        pltpu.VMEM((2, t, t), F32), pltpu.VMEM((2, t, t), F32),
                        pltpu.VMEM((nq, 2, V_ROWS, t), F32),
                        pltpu.VMEM((nq, 2, 1, t), F32)],
        compiler_params=pltpu.CompilerParams(
            dimension_semantics=("arbitrary", "arbitrary"), vmem_limit_bytes=VMEM_LIMIT),
        name="diff_attention",
    )(qt, kk3, vt, bias, lam_p, sw_col)


def _head_major(w):
    d_in = w.shape[0]
    return w.reshape(d_in, 2, HEADS, HALF_W).transpose(0, 2, 1, 3).reshape(d_in, D_MODEL)


def kernel(x, norm_w, ffn_w_in, ffn_w_out, hgrn_w_in, hgrn_lower_bounds, hgrn_gnorm_w, hgrn_w_out,
           kv_norm_w, w_kv, rel_bias, diff_w_q, diff_lambda, diff_subln_w, diff_w_out, final_norm_w):
    b, s, d = x.shape
    assert d == D_MODEL and s % ATT_TILE == 0 and s % FFN_ROWS == 0 and s % (HGRN_TILES * HGRN_ROWS) == 0
    assert ATT_TILE >= MAX_DISTANCE
    n = b * s

    def ffn_w(l, j):
        w_in = ffn_w_in[l, j].astype(BF16)
        return (norm_w[l, 2 * j].reshape(1, d), w_in[:, :D_FF], w_in[:, D_FF:], ffn_w_out[l, j].astype(BF16))

    wkv = jnp.concatenate([_head_major(w_kv[:, :D_MODEL]), w_kv[:, D_MODEL:]], axis=1).astype(BF16)
    wq = _head_major(diff_w_q[0]).astype(BF16)
    lambda_init = 0.8 - 0.6 * math.exp(-0.3 * 1)

    bias = _bias_tiles(rel_bias)

    h = _ffn(x.reshape(n, d), *ffn_w(0, 0), batch=b)
    h = _hgrn(h.reshape(b, s, d), norm_w[0, 1].reshape(1, d), hgrn_w_in[0].astype(BF16),
              hgrn_lower_bounds, hgrn_gnorm_w[0].reshape(1, HEAD_W), hgrn_w_out[0].astype(BF16))
    h, kk, vt = _ffn(h.reshape(n, d), *ffn_w(0, 1), batch=b, epilogue="kv",
                     epi_args=(kv_norm_w.reshape(1, d), wkv))
    h, qt = _ffn(h, *ffn_w(1, 0), batch=b, epilogue="q", epi_args=(norm_w[1, 1].reshape(1, d), wq))
    o = _attention(qt, kk.reshape(b, s, d), vt, bias, diff_lambda[0],
                   diff_subln_w[0].reshape(HEAD_W, 1), lambda_init=lambda_init)
    out = _ffn(h, *ffn_w(1, 1), batch=b, prologue="attn_out",
               pro_args=(o.reshape(n, d), diff_w_out[0].astype(BF16)),
               epilogue="final", epi_args=(final_norm_w.reshape(1, d),))
    return out.reshape(b, s, d)
```

```python
import functools
import math

import numpy as np
import jax
import jax.numpy as jnp
from jax import lax
from jax.experimental import pallas as pl
from jax.experimental.pallas import tpu as pltpu

F32 = jnp.float32
BF16 = jnp.bfloat16

D_MODEL = 1024
D_FF = 2816
EPS = 1e-6
FFN_RES = 0.5
HEADS = 8
HEAD_W = D_MODEL // HEADS
HALF_W = HEAD_W // 2
HGRN_CHUNK = 64
N_BUCKETS = 32
MAX_DISTANCE = 128
LOG2E = math.log2(math.e)
NEG = -1e30

FFN_ROWS = 512
FFN_COLS = 256
HGRN_ROWS = 256
HGRN_TILES = 4
ATT_TILE = 512
V_ROWS = HEAD_W + 16
VMEM_LIMIT = 56 * 1024 * 1024

NT_DIMS = (((1,), (1,)), ((), ()))
TN_DIMS = (((0,), (0,)), ((), ()))


def _dot(a, b):
    return jnp.dot(a, b, preferred_element_type=F32)


def _rms(x, w):
    return x * lax.rsqrt(jnp.mean(x * x, axis=-1, keepdims=True) + EPS) * w


def _silu(x):
    return x * jax.nn.sigmoid(x)


def _resident(shape):
    zeros = (0,) * len(shape)
    return pl.BlockSpec(shape, lambda *_: zeros, pipeline_mode=pl.Buffered(1))


def _ffn_kernel(*refs, prologue, epilogue):
    it = iter(refs)
    h_ref = next(it)
    if prologue == "attn_out":
        o_ref, wo_ref = next(it), next(it)
    nw_ref, win_ref, wd_ref = next(it), next(it), next(it)
    if epilogue is not None:
        enw_ref = next(it)
    if epilogue in ("kv", "q"):
        ew_ref = next(it)
    outs = [next(it) for _ in range({None: 1, "final": 1, "q": 2, "kv": 3}[epilogue])]
    act_ref = next(it)

    h = h_ref[...]
    if prologue == "attn_out":
        h = h + _dot(o_ref[...], wo_ref[...])
    xn = _rms(h, nw_ref[...]).astype(BF16)
    for c in range(D_FF // FFN_COLS):
        sl = slice(c * FFN_COLS, (c + 1) * FFN_COLS)
        g = _dot(xn, win_ref[:, sl])
        u = _dot(xn, win_ref[:, D_FF + c * FFN_COLS:D_FF + (c + 1) * FFN_COLS])
        act_ref[:, sl] = (_silu(g) * u).astype(BF16)
    h = h + FFN_RES * _dot(act_ref[...], wd_ref[...])

    if epilogue == "final":
        outs[0][...] = _rms(h, enw_ref[...])
        return
    outs[0][...] = h
    if epilogue == "kv":
        kv = _dot(_rms(h, enw_ref[...]).astype(BF16), ew_ref[...])
        outs[1][...] = kv[:, :D_MODEL].astype(BF16)
        vt = kv[:, D_MODEL:].T.astype(BF16)
        for hh in range(HEADS):
            outs[2][0, hh * V_ROWS:hh * V_ROWS + HEAD_W, :] = vt[hh * HEAD_W:(hh + 1) * HEAD_W]
            outs[2][0, hh * V_ROWS + HEAD_W:(hh + 1) * V_ROWS, :] = jnp.ones((V_ROWS - HEAD_W, vt.shape[1]), BF16)
    elif epilogue == "q":
        q = _dot(_rms(h, enw_ref[...]).astype(BF16), ew_ref[...])
        outs[1][0] = (q * (HALF_W ** -0.5 * LOG2E)).T.astype(BF16)


def _ffn(h, nw, win, wd, *, batch, prologue=None, pro_args=(), epilogue=None, epi_args=()):
    n = h.shape[0]
    seq = n // batch
    tm = FFN_ROWS
    per_seq = seq // tm
    row = pl.BlockSpec((tm, D_MODEL), lambda i: (i, 0))
    tr = pl.BlockSpec((1, D_MODEL, tm), lambda i: (i // per_seq, 0, i % per_seq))
    in_specs, args = [row], [h]
    if prologue == "attn_out":
        in_specs += [row, _resident((D_MODEL, D_MODEL))]
        args += list(pro_args)
    in_specs += [_resident((1, D_MODEL)), _resident((D_MODEL, 2 * D_FF)), _resident((D_FF, D_MODEL))]
    args += [nw, win, wd]
    if epilogue is not None:
        in_specs.append(_resident((1, D_MODEL)))
    if epilogue in ("kv", "q"):
        in_specs.append(_resident(epi_args[1].shape))
    args += list(epi_args)
    h_shape = jax.ShapeDtypeStruct((n, D_MODEL), F32)
    t_shape = jax.ShapeDtypeStruct((batch, D_MODEL, seq), BF16)
    if epilogue in (None, "final"):
        out_shape, out_specs = h_shape, row
    elif epilogue == "q":
        out_shape, out_specs = (h_shape, t_shape), (row, tr)
    else:
        out_shape = (h_shape, jax.ShapeDtypeStruct((n, D_MODEL), BF16),
                     jax.ShapeDtypeStruct((batch, HEADS * V_ROWS, seq), BF16))
        out_specs = (row, row, pl.BlockSpec((1, HEADS * V_ROWS, tm), tr.index_map))
    return pl.pallas_call(
        functools.partial(_ffn_kernel, prologue=prologue, epilogue=epilogue),
        out_shape=out_shape,
        grid=(n // tm,),
        in_specs=in_specs,
        out_specs=out_specs,
        scratch_shapes=[pltpu.VMEM((tm, D_FF), BF16)],
        compiler_params=pltpu.CompilerParams(
            dimension_semantics=("arbitrary",), vmem_limit_bytes=VMEM_LIMIT),
        name="ffn_" + str(prologue) + "_" + str(epilogue),
    )(*args)


def _hgrn_kernel(h_ref, nw_ref, win_ref, lbraw_ref, ltri_ref, gw_ref, wout_ref, out_ref,
                 z_ref, st_ref, o_ref, og_ref):
    t = HGRN_ROWS
    n_buf = z_ref.shape[0]

    @pl.when(pl.program_id(1) == 0)
    def _():
        st_ref[...] = jnp.zeros_like(st_ref)

    def project(i):
        h = h_ref[0, i * t:(i + 1) * t]
        z_ref[i % n_buf] = _dot(_rms(h, nw_ref[...]).astype(BF16), win_ref[...])

    project(0)
    for i in range(HGRN_TILES):
        if i + 1 < HGRN_TILES:
            project(i + 1)
        out_ref[0, i * t:(i + 1) * t] = _hgrn_mix(
            h_ref[0, i * t:(i + 1) * t], z_ref.at[i % n_buf], o_ref.at[i % 2], og_ref.at[i % 2],
            lbraw_ref=lbraw_ref, ltri_ref=ltri_ref, gw_ref=gw_ref, wout_ref=wout_ref, st_ref=st_ref)


def _hgrn_mix(h, z_ref, o_ref, og_ref, *, lbraw_ref, ltri_ref, gw_ref, wout_ref, st_ref):
    t = h.shape[0]
    c = HGRN_CHUNK
    q = _silu(z_ref[:, :D_MODEL])
    lbr = lbraw_ref[...]
    e = jnp.exp(lbr - jnp.max(lbr, axis=0, keepdims=True))
    lb = e[0:1] / jnp.sum(e, axis=0, keepdims=True)
    f = lb + (1.0 - lb) * jax.nn.sigmoid(z_ref[:, D_MODEL:2 * D_MODEL])
    k = 1.0 - f
    logf = jnp.log(f)
    hi = logf.astype(BF16)
    lo = (logf - hi.astype(F32)).astype(BF16)
    ltri = ltri_ref[...]
    bc = _dot(ltri, hi) + _dot(ltri, lo)
    v = z_ref[:, 2 * D_MODEL:3 * D_MODEL]

    tril = lax.broadcasted_iota(jnp.int32, (c, c), 0) >= lax.broadcasted_iota(jnp.int32, (c, c), 1)
    for n in range(t // c):
        r = slice(n * c, (n + 1) * c)
        bcn = bc[r]
        b_last = bcn[c - 1:c]
        b_mid = bcn[c // 2 - 1:c // 2]
        qn, kn = q[r], k[r]
        qd = (qn * jnp.exp(bcn)).astype(BF16)
        kd = (kn * jnp.exp(b_last - bcn)).astype(BF16)
        qi = (qn * jnp.exp(bcn - b_mid)).astype(BF16)
        ki = (kn * jnp.exp(b_mid - bcn)).astype(BF16)
        dec = jnp.exp(b_last)
        vb = v[r].astype(BF16)
        for hh in range(HEADS):
            cs = slice(hh * HEAD_W, (hh + 1) * HEAD_W)
            a = lax.dot_general(qi[:, cs], ki[:, cs], NT_DIMS, preferred_element_type=F32)
            a = jnp.where(tril, a, 0.0).astype(BF16)
            st = st_ref[hh]
            o_ref[r, cs] = _dot(a, vb[:, cs]) + lax.dot_general(
                qd[:, cs], st.astype(BF16), NT_DIMS, preferred_element_type=F32)
            ut = lax.dot_general(vb[:, cs], kd[:, cs], TN_DIMS, preferred_element_type=F32)
            st_ref[hh] = st * dec[:, cs] + ut

    gate = _silu(z_ref[:, 3 * D_MODEL:])
    gw = gw_ref[...]
    for hh in range(HEADS):
        cs = slice(hh * HEAD_W, (hh + 1) * HEAD_W)
        og_ref[:, cs] = (_rms(o_ref[:, cs], gw) * gate[:, cs]).astype(BF16)
    return h + _dot(og_ref[...], wout_ref[...])


def _hgrn(h3, nw, win, lbraw, gw, wout):
    b, s, _ = h3.shape
    t = HGRN_ROWS
    rows = HGRN_TILES * t
    idx = np.arange(t)
    ltri = jnp.asarray((idx[:, None] // HGRN_CHUNK == idx[None, :] // HGRN_CHUNK)
                       & (idx[None, :] <= idx[:, None]), BF16)
    blk = pl.BlockSpec((1, rows, D_MODEL), lambda i, j: (i, j, 0))
    return pl.pallas_call(
        _hgrn_kernel,
        out_shape=jax.ShapeDtypeStruct(h3.shape, F32),
        grid=(b, s // rows),
        in_specs=[blk, _resident((1, D_MODEL)), _resident(win.shape), _resident(lbraw.shape),
                  _resident((t, t)), _resident((1, HEAD_W)), _resident((D_MODEL, D_MODEL))],
        out_specs=blk,
        scratch_shapes=[pltpu.VMEM((min(3, HGRN_TILES), t, 4 * D_MODEL), F32),
                        pltpu.VMEM((HEADS, HEAD_W, HEAD_W), F32),
                        pltpu.VMEM((2, t, D_MODEL), F32),
                        pltpu.VMEM((2, t, D_MODEL), BF16)],
        compiler_params=pltpu.CompilerParams(
            dimension_semantics=("arbitrary", "arbitrary"), vmem_limit_bytes=VMEM_LIMIT),
        name="hgrn_layer",
    )(h3, nw, win, lbraw, ltri, gw, wout)


def _bucket_starts():
    n = np.arange(4 * MAX_DISTANCE, dtype=np.int64)
    max_exact = N_BUCKETS // 2
    nf = np.maximum(n, 1).astype(np.float32)
    large = max_exact + (np.log(nf / np.float32(max_exact)) / np.float32(math.log(MAX_DISTANCE / max_exact))
                         * np.float32(N_BUCKETS - max_exact)).astype(np.int32)
    bucket = np.where(n < max_exact, n, np.minimum(large, N_BUCKETS - 1))
    assert np.all(np.diff(bucket) >= 0) and bucket[-1] == N_BUCKETS - 1
    return [int(np.argmax(bucket >= b)) for b in range(N_BUCKETS)]


def _bias_kernel(tab_ref, out_ref, *, starts):
    hh, w = pl.program_id(0), pl.program_id(1)
    t = out_ref.shape[2]
    d = (lax.broadcasted_iota(jnp.int32, (t, t), 1) - lax.broadcasted_iota(jnp.int32, (t, t), 0)) + w * t
    val = jnp.full((t, t), tab_ref[0, hh], F32)
    for b in range(1, N_BUCKETS):
        val = jnp.where(d >= starts[b], tab_ref[b, hh], val)
    out_ref[0, 0] = jnp.where(d >= 0, (val - tab_ref[N_BUCKETS - 1, hh]) * LOG2E, NEG)


def _bias_tiles(rel_bias):
    t = ATT_TILE
    return pl.pallas_call(
        functools.partial(_bias_kernel, starts=_bucket_starts()),
        out_shape=jax.ShapeDtypeStruct((HEADS, 2, t, t), F32),
        grid=(HEADS, 2),
        in_specs=[pl.BlockSpec(memory_space=pltpu.SMEM)],
        out_specs=pl.BlockSpec((1, 1, t, t), lambda i, j: (i, j, 0, 0)),
        compiler_params=pltpu.CompilerParams(dimension_semantics=("arbitrary", "arbitrary")),
        name="t5_bias_tiles",
    )(rel_bias)


def _attn_kernel(qt_ref, kk_ref, vt_ref, bias_ref, lam_ref, sw_ref, o_ref,
                 qa_ref, qb_ref, sa_ref, sb_ref, acc_ref, m_ref, *, lambda_init):
    t = ATT_TILE
    nq = kk_ref.shape[1] // t
    n_far = (nq - 1) * (nq - 2) // 2
    unroll = 8

    row = lax.broadcasted_iota(jnp.int32, qt_ref.shape[1:], 0)
    qt = qt_ref[0]
    qa_ref[...] = jnp.where(row < HALF_W, qt, jnp.zeros_like(qt))
    qb_ref[...] = jnp.where(row >= HALF_W, qt, jnp.zeros_like(qt))
    acc_ref[...] = jnp.zeros_like(acc_ref)
    m_ref[...] = jnp.full_like(m_ref, NEG)

    def scores(qi, ki, dst):
        cols = pl.ds(pl.multiple_of(qi * t, t), t)
        kt = kk_ref[0, pl.ds(pl.multiple_of(ki * t, t), t), :]
        dst[0] = _dot(kt, qa_ref[:, cols])
        dst[1] = _dot(kt, qb_ref[:, cols])

    def accumulate(qi, ki, src, near):
        vt = vt_ref[0, :, pl.ds(pl.multiple_of(ki * t, t), t)]
        for mp in range(2):
            if near == 0:
                s = src[mp] + bias_ref[0, 0]
            elif near == 1:
                s = jnp.concatenate([src[mp, :t - MAX_DISTANCE],
                                     src[mp, t - MAX_DISTANCE:] + bias_ref[0, 1, t - MAX_DISTANCE:]], axis=0)
            else:
                s = src[mp]
            m_old = m_ref[qi, mp]
            m_new = jnp.maximum(m_old, jnp.max(s, axis=0, keepdims=True))
            p = jnp.exp2(s - m_new).astype(BF16)
            acc_ref[qi, mp] = jnp.exp2(m_old - m_new) * acc_ref[qi, mp] + _dot(vt, p)
            m_ref[qi, mp] = m_new

    lp = lam_ref[...]
    lam = (jnp.exp(jnp.sum(lp[0:1] * lp[1:2], axis=1, keepdims=True))
           - jnp.exp(jnp.sum(lp[2:3] * lp[3:4], axis=1, keepdims=True)) + lambda_init)

    def finalize(qi):
        a1, a2 = acc_ref[qi, 0], acc_ref[qi, 1]
        ot = (a1[:HEAD_W] * (1.0 / a1[HEAD_W:HEAD_W + 1])
              - lam * (a2[:HEAD_W] * (1.0 / a2[HEAD_W:HEAD_W + 1])))
        ms = jnp.mean(ot * ot, axis=0, keepdims=True)
        y = ot * lax.rsqrt(ms + EPS) * sw_ref[...] * (1.0 - lambda_init)
        o_ref[0, pl.ds(pl.multiple_of(qi * t, t), t), :] = y.T.astype(BF16)

    def far_step(c, src, dst):
        qi, ki = c
        wrap = ki >= qi - 2
        last = (qi == nq - 1) & wrap
        qn = jnp.where(last, 0, jnp.where(wrap, qi + 1, qi))
        kn = jnp.where(wrap, 0, ki + 1)
        scores(qn, kn, dst)
        accumulate(qi, ki, src, None)
        return qn, kn

    def near_tiles(qi, count, src, dst):
        for u in range(count):
            q = qi + u
            scores(q, q, dst)
            accumulate(q, q - 1, src, 1)
            qn = jnp.minimum(q + 1, nq - 1)
            scores(qn, qn - 1, src)
            accumulate(q, q, dst, 0)
            finalize(q)

    bufs = (sa_ref, sb_ref)
    if n_far > 0:
        scores(2, 0, sa_ref)

        def far_group(_, c):
            for u in range(unroll):
                c = far_step(c, bufs[u % 2], bufs[(u + 1) % 2])
            return c

        c = lax.fori_loop(0, n_far // unroll, far_group, (jnp.int32(2), jnp.int32(0)))
        for u in range(n_far % unroll):
            c = far_step(c, bufs[u % 2], bufs[(u + 1) % 2])
    else:
        scores(0, 0, sa_ref)
    src, dst = bufs[n_far % 2], bufs[(n_far + 1) % 2]
    scores(min(1, nq - 1), 0, dst)
    accumulate(0, 0, src, 0)
    finalize(0)
    per_group = unroll // 2

    def near_group(g, carry):
        near_tiles(1 + g * per_group, per_group, dst, src)
        return carry

    lax.fori_loop(0, (nq - 1) // per_group, near_group, 0)
    rest = (nq - 1) % per_group
    if rest:
        near_tiles(nq - rest, rest, dst, src)


def _attention(qt, kk3, vt, bias, lam_p, sw_col, *, lambda_init):
    b, s, _ = kk3.shape
    t = ATT_TILE
    nq = s // t
    return pl.pallas_call(
        functools.partial(_attn_kernel, lambda_init=lambda_init),
        out_shape=jax.ShapeDtypeStruct((b, s, D_MODEL), BF16),
        grid=(b, HEADS),
        in_specs=[
            pl.BlockSpec((1, HEAD_W, s), lambda i, j: (i, j, 0)),
            pl.BlockSpec((1, s, HEAD_W), lambda i, j: (i, 0, j)),
            pl.BlockSpec((1, V_ROWS, s), lambda i, j: (i, j, 0)),
            pl.BlockSpec((1, 2, t, t), lambda i, j: (j, 0, 0, 0)),
            _resident(lam_p.shape),
            _resident((HEAD_W, 1)),
        ],
        out_specs=pl.BlockSpec((1, s, HEAD_W), lambda i, j: (i, 0, j)),
        scratch_shapes=[pltpu.VMEM((HEAD_W, s), BF16), pltpu.VMEM((HEAD_W, s), BF16),
                        pltpu.VMEM((2, t, t), F32), pltpu.VMEM((2, t, t), F32),
                        pltpu.VMEM((nq, 2, V_ROWS, t), F32),
                        pltpu.VMEM((nq, 2, 1, t), F32)],
        compiler_params=pltpu.CompilerParams(
            dimension_semantics=("arbitrary", "arbitrary"), vmem_limit_bytes=VMEM_LIMIT),
        name="diff_attention",
    )(qt, kk3, vt, bias, lam_p, sw_col)


def _head_major(w):
    d_in = w.shape[0]
    return w.reshape(d_in, 2, HEADS, HALF_W).transpose(0, 2, 1, 3).reshape(d_in, D_MODEL)


def kernel(x, norm_w, ffn_w_in, ffn_w_out, hgrn_w_in, hgrn_lower_bounds, hgrn_gnorm_w, hgrn_w_out,
           kv_norm_w, w_kv, rel_bias, diff_w_q, diff_lambda, diff_subln_w, diff_w_out, final_norm_w):
    b, s, d = x.shape
    assert d == D_MODEL and s % ATT_TILE == 0 and s % FFN_ROWS == 0 and s % (HGRN_TILES * HGRN_ROWS) == 0
    assert ATT_TILE >= MAX_DISTANCE
    n = b * s

    def ffn_w(l, j):
        return (norm_w[l, 2 * j].reshape(1, d), ffn_w_in[l, j].astype(BF16), ffn_w_out[l, j].astype(BF16))

    wkv = jnp.concatenate([_head_major(w_kv[:, :D_MODEL]), w_kv[:, D_MODEL:]], axis=1).astype(BF16)
    wq = _head_major(diff_w_q[0]).astype(BF16)
    lambda_init = 0.8 - 0.6 * math.exp(-0.3 * 1)

    bias = _bias_tiles(rel_bias)

    h = _ffn(x.reshape(n, d), *ffn_w(0, 0), batch=b)
    h = _hgrn(h.reshape(b, s, d), norm_w[0, 1].reshape(1, d), hgrn_w_in[0].astype(BF16),
              hgrn_lower_bounds, hgrn_gnorm_w[0].reshape(1, HEAD_W), hgrn_w_out[0].astype(BF16))
    h, kk, vt = _ffn(h.reshape(n, d), *ffn_w(0, 1), batch=b, epilogue="kv",
                     epi_args=(kv_norm_w.reshape(1, d), wkv))
    h, qt = _ffn(h, *ffn_w(1, 0), batch=b, epilogue="q", epi_args=(norm_w[1, 1].reshape(1, d), wq))
    o = _attention(qt, kk.reshape(b, s, d), vt, bias, diff_lambda[0],
                   diff_subln_w[0].reshape(HEAD_W, 1), lambda_init=lambda_init)
    out = _ffn(h, *ffn_w(1, 1), batch=b, prologue="attn_out",
               pro_args=(o.reshape(n, d), diff_w_out[0].astype(BF16)),
               epilogue="final", epi_args=(final_norm_w.reshape(1, d),))
    return out.reshape(b, s, d)
```

```python
import functools
import math

import numpy as np
import jax
import jax.numpy as jnp
from jax import lax
from jax.experimental import pallas as pl
from jax.experimental.pallas import tpu as pltpu

F32 = jnp.float32
BF16 = jnp.bfloat16

D_MODEL = 1024
D_FF = 2816
EPS = 1e-6
FFN_RES = 0.5
HEADS = 8
HEAD_W = D_MODEL // HEADS
HALF_W = HEAD_W // 2
HGRN_CHUNK = 64
N_BUCKETS = 32
MAX_DISTANCE = 128
LOG2E = math.log2(math.e)
NEG = -1e30

FFN_ROWS = 512
FFN_COLS = 256
HGRN_ROWS = 256
HGRN_TILES = 4
ATT_TILE = 512
V_ROWS = HEAD_W + 16
VMEM_LIMIT = 56 * 1024 * 1024

NT_DIMS = (((1,), (1,)), ((), ()))
TN_DIMS = (((0,), (0,)), ((), ()))


def _dot(a, b):
    return jnp.dot(a, b, preferred_element_type=F32)


def _rms(x, w):
    return x * lax.rsqrt(jnp.mean(x * x, axis=-1, keepdims=True) + EPS) * w


def _silu(x):
    return x * jax.nn.sigmoid(x)


def _resident(shape):
    zeros = (0,) * len(shape)
    return pl.BlockSpec(shape, lambda *_: zeros, pipeline_mode=pl.Buffered(1))


def _ffn_kernel(*refs, prologue, epilogue):
    it = iter(refs)
    h_ref = next(it)
    if prologue == "attn_out":
        o_ref, wo_ref = next(it), next(it)
    nw_ref, win_ref, wd_ref = next(it), next(it), next(it)
    if epilogue is not None:
        enw_ref = next(it)
    if epilogue in ("kv", "q"):
        ew_ref = next(it)
    outs = [next(it) for _ in range({None: 1, "final": 1, "q": 2, "kv": 3}[epilogue])]
    act_ref = next(it)

    h = h_ref[...]
    if prologue == "attn_out":
        h = h + _dot(o_ref[...], wo_ref[...])
    xn = _rms(h, nw_ref[...]).astype(BF16)
    for c in range(D_FF // FFN_COLS):
        sl = slice(c * FFN_COLS, (c + 1) * FFN_COLS)
        g = _dot(xn, win_ref[:, sl])
        u = _dot(xn, win_ref[:, D_FF + c * FFN_COLS:D_FF + (c + 1) * FFN_COLS])
        act_ref[:, sl] = (_silu(g) * u).astype(BF16)
    h = h + FFN_RES * _dot(act_ref[...], wd_ref[...])

    if epilogue == "final":
        outs[0][...] = _rms(h, enw_ref[...])
        return
    outs[0][...] = h
    if epilogue == "kv":
        kv = _dot(_rms(h, enw_ref[...]).astype(BF16), ew_ref[...])
        outs[1][...] = kv[:, :D_MODEL].astype(BF16)
        vt = kv[:, D_MODEL:].T.astype(BF16)
        for hh in range(HEADS):
            outs[2][0, hh * V_ROWS:hh * V_ROWS + HEAD_W, :] = vt[hh * HEAD_W:(hh + 1) * HEAD_W]
            outs[2][0, hh * V_ROWS + HEAD_W:(hh + 1) * V_ROWS, :] = jnp.ones((V_ROWS - HEAD_W, vt.shape[1]), BF16)
    elif epilogue == "q":
        q = _dot(_rms(h, enw_ref[...]).astype(BF16), ew_ref[...])
        outs[1][0] = (q * (HALF_W ** -0.5 * LOG2E)).T.astype(BF16)


def _ffn(h, nw, win_all, wd_all, which, *, batch, prologue=None, pro_args=(), epilogue=None, epi_args=()):
    n = h.shape[0]
    seq = n // batch
    tm = FFN_ROWS
    per_seq = seq // tm
    row = pl.BlockSpec((tm, D_MODEL), lambda i: (i, 0))
    tr = pl.BlockSpec((1, D_MODEL, tm), lambda i: (i // per_seq, 0, i % per_seq))
    in_specs, args = [row], [h]
    if prologue == "attn_out":
        in_specs += [row, _resident((D_MODEL, D_MODEL))]
        args += list(pro_args)
    in_specs += [_resident((1, D_MODEL)),
                 pl.BlockSpec((None, None, D_MODEL, 2 * D_FF), lambda i: (*which, 0, 0), pipeline_mode=pl.Buffered(1)),
                 pl.BlockSpec((None, None, D_FF, D_MODEL), lambda i: (*which, 0, 0), pipeline_mode=pl.Buffered(1))]
    args += [nw, win_all, wd_all]
    if epilogue is not None:
        in_specs.append(_resident((1, D_MODEL)))
    if epilogue in ("kv", "q"):
        in_specs.append(_resident(epi_args[1].shape))
    args += list(epi_args)
    h_shape = jax.ShapeDtypeStruct((n, D_MODEL), F32)
    t_shape = jax.ShapeDtypeStruct((batch, D_MODEL, seq), BF16)
    if epilogue in (None, "final"):
        out_shape, out_specs = h_shape, row
    elif epilogue == "q":
        out_shape, out_specs = (h_shape, t_shape), (row, tr)
    else:
        out_shape = (h_shape, jax.ShapeDtypeStruct((n, D_MODEL), BF16),
                     jax.ShapeDtypeStruct((batch, HEADS * V_ROWS, seq), BF16))
        out_specs = (row, row, pl.BlockSpec((1, HEADS * V_ROWS, tm), tr.index_map))
    return pl.pallas_call(
        functools.partial(_ffn_kernel, prologue=prologue, epilogue=epilogue),
        out_shape=out_shape,
        grid=(n // tm,),
        in_specs=in_specs,
        out_specs=out_specs,
        scratch_shapes=[pltpu.VMEM((tm, D_FF), BF16)],
        compiler_params=pltpu.CompilerParams(
            dimension_semantics=("arbitrary",), vmem_limit_bytes=VMEM_LIMIT),
        name="ffn_" + str(prologue) + "_" + str(epilogue),
    )(*args)


def _hgrn_kernel(h_ref, nw_ref, win_ref, lbraw_ref, ltri_ref, gw_ref, wout_ref, out_ref,
                 z_ref, st_ref, o_ref, og_ref):
    t = HGRN_ROWS
    n_buf = z_ref.shape[0]

    @pl.when(pl.program_id(1) == 0)
    def _():
        st_ref[...] = jnp.zeros_like(st_ref)

    def project(i):
        h = h_ref[0, i * t:(i + 1) * t]
        z_ref[i % n_buf] = _dot(_rms(h, nw_ref[...]).astype(BF16), win_ref[...])

    project(0)
    for i in range(HGRN_TILES):
        if i + 1 < HGRN_TILES:
            project(i + 1)
        out_ref[0, i * t:(i + 1) * t] = _hgrn_mix(
            h_ref[0, i * t:(i + 1) * t], z_ref.at[i % n_buf], o_ref.at[i % 2], og_ref.at[i % 2],
            lbraw_ref=lbraw_ref, ltri_ref=ltri_ref, gw_ref=gw_ref, wout_ref=wout_ref, st_ref=st_ref)


def _hgrn_mix(h, z_ref, o_ref, og_ref, *, lbraw_ref, ltri_ref, gw_ref, wout_ref, st_ref):
    t = h.shape[0]
    c = HGRN_CHUNK
    q = _silu(z_ref[:, :D_MODEL])
    lbr = lbraw_ref[...]
    e = jnp.exp(lbr - jnp.max(lbr, axis=0, keepdims=True))
    lb = e[0:1] / jnp.sum(e, axis=0, keepdims=True)
    f = lb + (1.0 - lb) * jax.nn.sigmoid(z_ref[:, D_MODEL:2 * D_MODEL])
    k = 1.0 - f
    logf = jnp.log(f)
    hi = logf.astype(BF16)
    lo = (logf - hi.astype(F32)).astype(BF16)
    ltri = ltri_ref[...]
    bc = _dot(ltri, hi) + _dot(ltri, lo)
    v = z_ref[:, 2 * D_MODEL:3 * D_MODEL]

    tril = lax.broadcasted_iota(jnp.int32, (c, c), 0) >= lax.broadcasted_iota(jnp.int32, (c, c), 1)
    for n in range(t // c):
        r = slice(n * c, (n + 1) * c)
        bcn = bc[r]
        b_last = bcn[c - 1:c]
        b_mid = bcn[c // 2 - 1:c // 2]
        qn, kn = q[r], k[r]
        qd = (qn * jnp.exp(bcn)).astype(BF16)
        kd = (kn * jnp.exp(b_last - bcn)).astype(BF16)
        qi = (qn * jnp.exp(bcn - b_mid)).astype(BF16)
        ki = (kn * jnp.exp(b_mid - bcn)).astype(BF16)
        dec = jnp.exp(b_last)
        vb = v[r].astype(BF16)
        for hh in range(HEADS):
            cs = slice(hh * HEAD_W, (hh + 1) * HEAD_W)
            a = lax.dot_general(qi[:, cs], ki[:, cs], NT_DIMS, preferred_element_type=F32)
            a = jnp.where(tril, a, 0.0).astype(BF16)
            st = st_ref[hh]
            o_ref[r, cs] = _dot(a, vb[:, cs]) + lax.dot_general(
                qd[:, cs], st.astype(BF16), NT_DIMS, preferred_element_type=F32)
            ut = lax.dot_general(vb[:, cs], kd[:, cs], TN_DIMS, preferred_element_type=F32)
            st_ref[hh] = st * dec[:, cs] + ut

    gate = _silu(z_ref[:, 3 * D_MODEL:])
    gw = gw_ref[...]
    for hh in range(HEADS):
        cs = slice(hh * HEAD_W, (hh + 1) * HEAD_W)
        og_ref[:, cs] = (_rms(o_ref[:, cs], gw) * gate[:, cs]).astype(BF16)
    return h + _dot(og_ref[...], wout_ref[...])


def _hgrn(h3, nw, win, lbraw, gw, wout):
    b, s, _ = h3.shape
    t = HGRN_ROWS
    rows = HGRN_TILES * t
    idx = np.arange(t)
    ltri = jnp.asarray((idx[:, None] // HGRN_CHUNK == idx[None, :] // HGRN_CHUNK)
                       & (idx[None, :] <= idx[:, None]), BF16)
    blk = pl.BlockSpec((1, rows, D_MODEL), lambda i, j: (i, j, 0))
    return pl.pallas_call(
        _hgrn_kernel,
        out_shape=jax.ShapeDtypeStruct(h3.shape, F32),
        grid=(b, s // rows),
        in_specs=[blk, _resident((1, D_MODEL)), _resident(win.shape), _resident(lbraw.shape),
                  _resident((t, t)), _resident((1, HEAD_W)), _resident((D_MODEL, D_MODEL))],
        out_specs=blk,
        scratch_shapes=[pltpu.VMEM((min(3, HGRN_TILES), t, 4 * D_MODEL), F32),
                        pltpu.VMEM((HEADS, HEAD_W, HEAD_W), F32),
                        pltpu.VMEM((2, t, D_MODEL), F32),
                        pltpu.VMEM((2, t, D_MODEL), BF16)],
        compiler_params=pltpu.CompilerParams(
            dimension_semantics=("arbitrary", "arbitrary"), vmem_limit_bytes=VMEM_LIMIT),
        name="hgrn_layer",
    )(h3, nw, win, lbraw, ltri, gw, wout)


def _bucket_starts():
    n = np.arange(4 * MAX_DISTANCE, dtype=np.int64)
    max_exact = N_BUCKETS // 2
    nf = np.maximum(n, 1).astype(np.float32)
    large = max_exact + (np.log(nf / np.float32(max_exact)) / np.float32(math.log(MAX_DISTANCE / max_exact))
                         * np.float32(N_BUCKETS - max_exact)).astype(np.int32)
    bucket = np.where(n < max_exact, n, np.minimum(large, N_BUCKETS - 1))
    assert np.all(np.diff(bucket) >= 0) and bucket[-1] == N_BUCKETS - 1
    return [int(np.argmax(bucket >= b)) for b in range(N_BUCKETS)]


def _bias_kernel(tab_ref, out_ref, *, starts):
    hh = pl.program_id(0)
    t = out_ref.shape[2]
    sb = MAX_DISTANCE
    far = tab_ref[N_BUCKETS - 1, hh]
    in_block = lax.broadcasted_iota(jnp.int32, (sb, sb), 1) - lax.broadcasted_iota(jnp.int32, (sb, sb), 0)
    for w in range(2):
        for r in range(t // sb):
            for c in range(t // sb):
                off = w * t + sb * (c - r)
                if off + (sb - 1) < 0:
                    blk = jnp.full((sb, sb), NEG, F32)
                elif off - (sb - 1) >= starts[N_BUCKETS - 1]:
                    blk = jnp.zeros((sb, sb), F32)
                else:
                    d = in_block + off
                    val = jnp.full((sb, sb), tab_ref[0, hh], F32)
                    for b in range(1, N_BUCKETS):
                        val = jnp.where(d >= starts[b], tab_ref[b, hh], val)
                    blk = jnp.where(d >= 0, (val - far) * LOG2E, NEG)
                out_ref[0, w, r * sb:(r + 1) * sb, c * sb:(c + 1) * sb] = blk


def _bias_tiles(rel_bias):
    t = ATT_TILE
    return pl.pallas_call(
        functools.partial(_bias_kernel, starts=_bucket_starts()),
        out_shape=jax.ShapeDtypeStruct((HEADS, 2, t, t), F32),
        grid=(HEADS,),
        in_specs=[pl.BlockSpec(memory_space=pltpu.SMEM)],
        out_specs=pl.BlockSpec((1, 2, t, t), lambda i: (i, 0, 0, 0)),
        compiler_params=pltpu.CompilerParams(dimension_semantics=("arbitrary",)),
        name="t5_bias_tiles",
    )(rel_bias)


def _attn_kernel(qt_ref, kk_ref, vt_ref, bias_ref, lam_ref, sw_ref, o_ref,
                 qa_ref, qb_ref, sa_ref, sb_ref, acc_ref, m_ref, *, lambda_init):
    t = ATT_TILE
    nq = kk_ref.shape[1] // t
    n_far = (nq - 1) * (nq - 2) // 2
    unroll = 8

    row = lax.broadcasted_iota(jnp.int32, qt_ref.shape[1:], 0)
    qt = qt_ref[0]
    qa_ref[...] = jnp.where(row < HALF_W, qt, jnp.zeros_like(qt))
    qb_ref[...] = jnp.where(row >= HALF_W, qt, jnp.zeros_like(qt))
    acc_ref[...] = jnp.zeros_like(acc_ref)
    m_ref[...] = jnp.full_like(m_ref, NEG)

    half = t // 2
    whole_block = ((0, t, t),)
    diag_block = ((0, half, half), (half, half, t))

    def scores(qi, ki, dst, pieces=whole_block):
        for mp, q_ref in enumerate((qa_ref, qb_ref)):
            for c0, nc, rows in pieces:
                kt = kk_ref[0, pl.ds(pl.multiple_of(ki * t, t), rows), :]
                qcols = q_ref[:, pl.ds(pl.multiple_of(qi * t + c0, nc), nc)]
                dst[mp, :rows, c0:c0 + nc] = _dot(kt, qcols)

    def accumulate(qi, ki, src, near):
        for mp in range(2):
            for c0, nc, rows in (diag_block if near == 0 else whole_block):
                cs = slice(c0, c0 + nc)
                vt = vt_ref[0, :, pl.ds(pl.multiple_of(ki * t, t), rows)]
                if near == 0:
                    s = src[mp, :rows, cs] + bias_ref[0, 0, :rows, cs]
                elif near == 1:
                    s = jnp.concatenate([src[mp, :t - MAX_DISTANCE],
                                         src[mp, t - MAX_DISTANCE:] + bias_ref[0, 1, t - MAX_DISTANCE:]], axis=0)
                else:
                    s = src[mp]
                m_old = m_ref[qi, mp, :, cs]
                m_new = jnp.maximum(m_old, jnp.max(s, axis=0, keepdims=True))
                p = jnp.exp2(s - m_new).astype(BF16)
                acc_ref[qi, mp, :, cs] = jnp.exp2(m_old - m_new) * acc_ref[qi, mp, :, cs] + _dot(vt, p)
                m_ref[qi, mp, :, cs] = m_new

    lp = lam_ref[...]
    lam = (jnp.exp(jnp.sum(lp[0:1] * lp[1:2], axis=1, keepdims=True))
           - jnp.exp(jnp.sum(lp[2:3] * lp[3:4], axis=1, keepdims=True)) + lambda_init)

    def finalize(qi):
        a1, a2 = acc_ref[qi, 0], acc_ref[qi, 1]
        ot = (a1[:HEAD_W] * (1.0 / a1[HEAD_W:HEAD_W + 1])
              - lam * (a2[:HEAD_W] * (1.0 / a2[HEAD_W:HEAD_W + 1])))
        ms = jnp.mean(ot * ot, axis=0, keepdims=True)
        y = ot * lax.rsqrt(ms + EPS) * sw_ref[...] * (1.0 - lambda_init)
        o_ref[0, pl.ds(pl.multiple_of(qi * t, t), t), :] = y.T.astype(BF16)

    def far_step(c, src, dst):
        qi, ki = c
        wrap = ki >= qi - 2
        last = (qi == nq - 1) & wrap
        qn = jnp.where(last, 0, jnp.where(wrap, qi + 1, qi))
        kn = jnp.where(wrap, 0, ki + 1)
        scores(qn, kn, dst)
        accumulate(qi, ki, src, None)
        return qn, kn

    def near_tiles(qi, count, src, dst):
        for u in range(count):
            q = qi + u
            scores(q, q, dst, diag_block)
            accumulate(q, q - 1, src, 1)
            qn = jnp.minimum(q + 1, nq - 1)
            scores(qn, qn - 1, src)
            accumulate(q, q, dst, 0)
            finalize(q)

    bufs = (sa_ref, sb_ref)
    if n_far > 0:
        scores(2, 0, sa_ref)

        def far_group(_, c):
            for u in range(unroll):
                c = far_step(c, bufs[u % 2], bufs[(u + 1) % 2])
            return c

        c = lax.fori_loop(0, n_far // unroll, far_group, (jnp.int32(2), jnp.int32(0)))
        for u in range(n_far % unroll):
            c = far_step(c, bufs[u % 2], bufs[(u + 1) % 2])
    else:
        scores(0, 0, sa_ref)
    src, dst = bufs[n_far % 2], bufs[(n_far + 1) % 2]
    scores(min(1, nq - 1), 0, dst)
    accumulate(0, 0, src, 0)
    finalize(0)
    per_group = unroll // 2

    def near_group(g, carry):
        near_tiles(1 + g * per_group, per_group, dst, src)
        return carry

    lax.fori_loop(0, (nq - 1) // per_group, near_group, 0)
    rest = (nq - 1) % per_group
    if rest:
        near_tiles(nq - rest, rest, dst, src)


def _attention(qt, kk3, vt, bias, lam_p, sw_col, *, lambda_init):
    b, s, _ = kk3.shape
    t = ATT_TILE
    nq = s // t
    return pl.pallas_call(
        functools.partial(_attn_kernel, lambda_init=lambda_init),
        out_shape=jax.ShapeDtypeStruct((b, s, D_MODEL), BF16),
        grid=(b, HEADS),
        in_specs=[
            pl.BlockSpec((1, HEAD_W, s), lambda i, j: (i, j, 0)),
            pl.BlockSpec((1, s, HEAD_W), lambda i, j: (i, 0, j)),
            pl.BlockSpec((1, V_ROWS, s), lambda i, j: (i, j, 0)),
            pl.BlockSpec((1, 2, t, t), lambda i, j: (j, 0, 0, 0)),
            _resident(lam_p.shape),
            _resident((HEAD_W, 1)),
        ],
        out_specs=pl.BlockSpec((1, s, HEAD_W), lambda i, j: (i, 0, j)),
        scratch_shapes=[pltpu.VMEM((HEAD_W, s), BF16), pltpu.VMEM((HEAD_W, s), BF16),
                        pltpu.VMEM((2, t, t), F32), pltpu.VMEM((2, t, t), F32),
                        pltpu.VMEM((nq, 2, V_ROWS, t), F32),
                        pltpu.VMEM((nq, 2, 1, t), F32)],
        compiler_params=pltpu.CompilerParams(
            dimension_semantics=("arbitrary", "arbitrary"), vmem_limit_bytes=VMEM_LIMIT),
        name="diff_attention",
    )(qt, kk3, vt, bias, lam_p, sw_col)


def _head_major(w):
    d_in = w.shape[0]
    return w.reshape(d_in, 2, HEADS, HALF_W).transpose(0, 2, 1, 3).reshape(d_in, D_MODEL)


def kernel(x, norm_w, ffn_w_in, ffn_w_out, hgrn_w_in, hgrn_lower_bounds, hgrn_gnorm_w, hgrn_w_out,
           kv_norm_w, w_kv, rel_bias, diff_w_q, diff_lambda, diff_subln_w, diff_w_out, final_norm_w):
    b, s, d = x.shape
    assert d == D_MODEL and s % ATT_TILE == 0 and s % FFN_ROWS == 0 and s % (HGRN_TILES * HGRN_ROWS) == 0
    assert ATT_TILE >= MAX_DISTANCE
    n = b * s

    w_in_all, w_out_all = ffn_w_in.astype(BF16), ffn_w_out.astype(BF16)

    def ffn_w(l, j):
        return norm_w[l, 2 * j].reshape(1, d), w_in_all, w_out_all, (l, j)

    wkv = jnp.concatenate([_head_major(w_kv[:, :D_MODEL]), w_kv[:, D_MODEL:]], axis=1).astype(BF16)
    wq = _head_major(diff_w_q[0]).astype(BF16)
    lambda_init = 0.8 - 0.6 * math.exp(-0.3 * 1)

    bias = _bias_tiles(rel_bias)

    h = _ffn(x.reshape(n, d), *ffn_w(0, 0), batch=b)
    h = _hgrn(h.reshape(b, s, d), norm_w[0, 1].reshape(1, d), hgrn_w_in[0].astype(BF16),
              hgrn_lower_bounds, hgrn_gnorm_w[0].reshape(1, HEAD_W), hgrn_w_out[0].astype(BF16))
    h, kk, vt = _ffn(h.reshape(n, d), *ffn_w(0, 1), batch=b, epilogue="kv",
                     epi_args=(kv_norm_w.reshape(1, d), wkv))
    h, qt = _ffn(h, *ffn_w(1, 0), batch=b, epilogue="q", epi_args=(norm_w[1, 1].reshape(1, d), wq))
    o = _attention(qt, kk.reshape(b, s, d), vt, bias, diff_lambda[0],
                   diff_subln_w[0].reshape(HEAD_W, 1), lambda_init=lambda_init)
    out = _ffn(h, *ffn_w(1, 1), batch=b, prologue="attn_out",
               pro_args=(o.reshape(n, d), diff_w_out[0].astype(BF16)),
               epilogue="final", epi_args=(final_norm_w.reshape(1, d),))
    return out.reshape(b, s, d)
```

```python
import functools
import math

import numpy as np
import jax
import jax.numpy as jnp
from jax import lax
from jax.experimental import pallas as pl
from jax.experimental.pallas import tpu as pltpu

F32 = jnp.float32
BF16 = jnp.bfloat16

D_MODEL = 1024
D_FF = 2816
EPS = 1e-6
FFN_RES = 0.5
HEADS = 8
HEAD_W = D_MODEL // HEADS
HALF_W = HEAD_W // 2
HGRN_CHUNK = 64
N_BUCKETS = 32
MAX_DISTANCE = 128
LOG2E = math.log2(math.e)
NEG = -1e30

FFN_ROWS = 1024
FFN_COLS = 256
HGRN_ROWS = 256
HGRN_TILES = 4
ATT_TILE = 512
V_ROWS = HEAD_W + 16
VMEM_LIMIT = 56 * 1024 * 1024

NT_DIMS = (((1,), (1,)), ((), ()))
TN_DIMS = (((0,), (0,)), ((), ()))


def _dot(a, b):
    return jnp.dot(a, b, preferred_element_type=F32)


def _rms(x, w):
    return x * lax.rsqrt(jnp.mean(x * x, axis=-1, keepdims=True) + EPS) * w


def _silu(x):
    return x * jax.nn.sigmoid(x)


def _resident(shape):
    zeros = (0,) * len(shape)
    return pl.BlockSpec(shape, lambda *_: zeros, pipeline_mode=pl.Buffered(1))


def _ffn_kernel(*refs, prologue, epilogue):
    it = iter(refs)
    h_ref = next(it)
    if prologue == "attn_out":
        o_ref, wo_ref = next(it), next(it)
    nw_ref, win_ref, wd_ref = next(it), next(it), next(it)
    if epilogue is not None:
        enw_ref = next(it)
    if epilogue in ("kv", "q"):
        ew_ref = next(it)
    outs = [next(it) for _ in range({None: 1, "final": 1, "q": 2, "kv": 3}[epilogue])]
    act_ref = next(it)

    h = h_ref[...]
    if prologue == "attn_out":
        h = h + _dot(o_ref[...], wo_ref[...])
    xn = _rms(h, nw_ref[...]).astype(BF16)
    for c in range(D_FF // FFN_COLS):
        sl = slice(c * FFN_COLS, (c + 1) * FFN_COLS)
        g = _dot(xn, win_ref[:, sl])
        u = _dot(xn, win_ref[:, D_FF + c * FFN_COLS:D_FF + (c + 1) * FFN_COLS])
        act_ref[:, sl] = (_silu(g) * u).astype(BF16)
    h = h + FFN_RES * _dot(act_ref[...], wd_ref[...])

    if epilogue == "final":
        outs[0][...] = _rms(h, enw_ref[...])
        return
    outs[0][...] = h
    if epilogue == "kv":
        kv = _dot(_rms(h, enw_ref[...]).astype(BF16), ew_ref[...])
        outs[1][...] = kv[:, :D_MODEL].astype(BF16)
        vt = kv[:, D_MODEL:].T.astype(BF16)
        for hh in range(HEADS):
            outs[2][0, hh * V_ROWS:hh * V_ROWS + HEAD_W, :] = vt[hh * HEAD_W:(hh + 1) * HEAD_W]
            outs[2][0, hh * V_ROWS + HEAD_W:(hh + 1) * V_ROWS, :] = jnp.ones((V_ROWS - HEAD_W, vt.shape[1]), BF16)
    elif epilogue == "q":
        q = _dot(_rms(h, enw_ref[...]).astype(BF16), ew_ref[...])
        outs[1][0] = (q * (HALF_W ** -0.5 * LOG2E)).T.astype(BF16)


def _ffn(h, nw, win_all, wd_all, which, *, batch, prologue=None, pro_args=(), epilogue=None, epi_args=()):
    n = h.shape[0]
    seq = n // batch
    tm = FFN_ROWS
    per_seq = seq // tm
    row = pl.BlockSpec((tm, D_MODEL), lambda i: (i, 0))
    tr = pl.BlockSpec((1, D_MODEL, tm), lambda i: (i // per_seq, 0, i % per_seq))
    in_specs, args = [row], [h]
    if prologue == "attn_out":
        in_specs += [row, _resident((D_MODEL, D_MODEL))]
        args += list(pro_args)
    in_specs += [_resident((1, D_MODEL)),
                 pl.BlockSpec((None, None, D_MODEL, 2 * D_FF), lambda i: (*which, 0, 0), pipeline_mode=pl.Buffered(1)),
                 pl.BlockSpec((None, None, D_FF, D_MODEL), lambda i: (*which, 0, 0), pipeline_mode=pl.Buffered(1))]
    args += [nw, win_all, wd_all]
    if epilogue is not None:
        in_specs.append(_resident((1, D_MODEL)))
    if epilogue in ("kv", "q"):
        in_specs.append(_resident(epi_args[1].shape))
    args += list(epi_args)
    h_shape = jax.ShapeDtypeStruct((n, D_MODEL), F32)
    t_shape = jax.ShapeDtypeStruct((batch, D_MODEL, seq), BF16)
    if epilogue in (None, "final"):
        out_shape, out_specs = h_shape, row
    elif epilogue == "q":
        out_shape, out_specs = (h_shape, t_shape), (row, tr)
    else:
        out_shape = (h_shape, jax.ShapeDtypeStruct((n, D_MODEL), BF16),
                     jax.ShapeDtypeStruct((batch, HEADS * V_ROWS, seq), BF16))
        out_specs = (row, row, pl.BlockSpec((1, HEADS * V_ROWS, tm), tr.index_map))
    return pl.pallas_call(
        functools.partial(_ffn_kernel, prologue=prologue, epilogue=epilogue),
        out_shape=out_shape,
        grid=(n // tm,),
        in_specs=in_specs,
        out_specs=out_specs,
        scratch_shapes=[pltpu.VMEM((tm, D_FF), BF16)],
        compiler_params=pltpu.CompilerParams(
            dimension_semantics=("arbitrary",), vmem_limit_bytes=VMEM_LIMIT),
        name="ffn_" + str(prologue) + "_" + str(epilogue),
    )(*args)


def _hgrn_kernel(h_ref, nw_ref, win_ref, lbraw_ref, ltri_ref, gw_ref, wout_ref, out_ref,
                 z_ref, st_ref, o_ref, og_ref):
    t = HGRN_ROWS
    n_buf = z_ref.shape[0]

    @pl.when(pl.program_id(1) == 0)
    def _():
        st_ref[...] = jnp.zeros_like(st_ref)

    def project(i):
        h = h_ref[0, i * t:(i + 1) * t]
        z_ref[i % n_buf] = _dot(_rms(h, nw_ref[...]).astype(BF16), win_ref[...])

    project(0)
    for i in range(HGRN_TILES):
        if i + 1 < HGRN_TILES:
            project(i + 1)
        out_ref[0, i * t:(i + 1) * t] = _hgrn_mix(
            h_ref[0, i * t:(i + 1) * t], z_ref.at[i % n_buf], o_ref.at[i % 2], og_ref.at[i % 2],
            lbraw_ref=lbraw_ref, ltri_ref=ltri_ref, gw_ref=gw_ref, wout_ref=wout_ref, st_ref=st_ref)


def _hgrn_mix(h, z_ref, o_ref, og_ref, *, lbraw_ref, ltri_ref, gw_ref, wout_ref, st_ref):
    t = h.shape[0]
    c = HGRN_CHUNK
    q = _silu(z_ref[:, :D_MODEL])
    lbr = lbraw_ref[...]
    e = jnp.exp(lbr - jnp.max(lbr, axis=0, keepdims=True))
    lb = e[0:1] / jnp.sum(e, axis=0, keepdims=True)
    f = lb + (1.0 - lb) * jax.nn.sigmoid(z_ref[:, D_MODEL:2 * D_MODEL])
    k = 1.0 - f
    logf = jnp.log(f)
    hi = logf.astype(BF16)
    lo = (logf - hi.astype(F32)).astype(BF16)
    ltri = ltri_ref[...]
    bc = _dot(ltri, hi) + _dot(ltri, lo)
    v = z_ref[:, 2 * D_MODEL:3 * D_MODEL]

    tril = lax.broadcasted_iota(jnp.int32, (c, c), 0) >= lax.broadcasted_iota(jnp.int32, (c, c), 1)
    for n in range(t // c):
        r = slice(n * c, (n + 1) * c)
        bcn = bc[r]
        b_last = bcn[c - 1:c]
        b_mid = bcn[c // 2 - 1:c // 2]
        qn, kn = q[r], k[r]
        qd = (qn * jnp.exp(bcn)).astype(BF16)
        kd = (kn * jnp.exp(b_last - bcn)).astype(BF16)
        qi = (qn * jnp.exp(bcn - b_mid)).astype(BF16)
        ki = (kn * jnp.exp(b_mid - bcn)).astype(BF16)
        dec = jnp.exp(b_last)
        vb = v[r].astype(BF16)
        for hh in range(HEADS):
            cs = slice(hh * HEAD_W, (hh + 1) * HEAD_W)
            a = lax.dot_general(qi[:, cs], ki[:, cs], NT_DIMS, preferred_element_type=F32)
            a = jnp.where(tril, a, 0.0).astype(BF16)
            st = st_ref[hh]
            o_ref[r, cs] = _dot(jnp.concatenate([qd[:, cs], a], axis=1),
                                jnp.concatenate([st.T.astype(BF16), vb[:, cs]], axis=0))
            ut = lax.dot_general(vb[:, cs], kd[:, cs], TN_DIMS, preferred_element_type=F32)
            st_ref[hh] = st * dec[:, cs] + ut

    gate = _silu(z_ref[:, 3 * D_MODEL:])
    gw = gw_ref[...]
    for hh in range(HEADS):
        cs = slice(hh * HEAD_W, (hh + 1) * HEAD_W)
        og_ref[:, cs] = (_rms(o_ref[:, cs], gw) * gate[:, cs]).astype(BF16)
    return h + _dot(og_ref[...], wout_ref[...])


def _hgrn(h3, nw, win, lbraw, gw, wout):
    b, s, _ = h3.shape
    t = HGRN_ROWS
    rows = HGRN_TILES * t
    idx = np.arange(t)
    ltri = jnp.asarray((idx[:, None] // HGRN_CHUNK == idx[None, :] // HGRN_CHUNK)
                       & (idx[None, :] <= idx[:, None]), BF16)
    blk = pl.BlockSpec((1, rows, D_MODEL), lambda i, j: (i, j, 0))
    return pl.pallas_call(
        _hgrn_kernel,
        out_shape=jax.ShapeDtypeStruct(h3.shape, F32),
        grid=(b, s // rows),
        in_specs=[blk, _resident((1, D_MODEL)), _resident(win.shape), _resident(lbraw.shape),
                  _resident((t, t)), _resident((1, HEAD_W)), _resident((D_MODEL, D_MODEL))],
        out_specs=blk,
        scratch_shapes=[pltpu.VMEM((min(3, HGRN_TILES), t, 4 * D_MODEL), F32),
                        pltpu.VMEM((HEADS, HEAD_W, HEAD_W), F32),
                        pltpu.VMEM((2, t, D_MODEL), F32),
                        pltpu.VMEM((2, t, D_MODEL), BF16)],
        compiler_params=pltpu.CompilerParams(
            dimension_semantics=("arbitrary", "arbitrary"), vmem_limit_bytes=VMEM_LIMIT),
        name="hgrn_layer",
    )(h3, nw, win, lbraw, ltri, gw, wout)


def _bucket_starts():
    n = np.arange(4 * MAX_DISTANCE, dtype=np.int64)
    max_exact = N_BUCKETS // 2
    nf = np.maximum(n, 1).astype(np.float32)
    large = max_exact + (np.log(nf / np.float32(max_exact)) / np.float32(math.log(MAX_DISTANCE / max_exact))
                         * np.float32(N_BUCKETS - max_exact)).astype(np.int32)
    bucket = np.where(n < max_exact, n, np.minimum(large, N_BUCKETS - 1))
    assert np.all(np.diff(bucket) >= 0) and bucket[-1] == N_BUCKETS - 1
    return [int(np.argmax(bucket >= b)) for b in range(N_BUCKETS)]


def _bias_kernel(tab_ref, out_ref, *, starts):
    hh = pl.program_id(0)
    t = out_ref.shape[2]
    sb = MAX_DISTANCE
    far = tab_ref[N_BUCKETS - 1, hh]
    in_block = lax.broadcasted_iota(jnp.int32, (sb, sb), 1) - lax.broadcasted_iota(jnp.int32, (sb, sb), 0)
    for w in range(2):
        for r in range(t // sb):
            for c in range(t // sb):
                off = w * t + sb * (c - r)
                if off + (sb - 1) < 0:
                    blk = jnp.full((sb, sb), NEG, F32)
                elif off - (sb - 1) >= starts[N_BUCKETS - 1]:
                    blk = jnp.zeros((sb, sb), F32)
                else:
                    d = in_block + off
                    val = jnp.full((sb, sb), tab_ref[0, hh], F32)
                    for b in range(1, N_BUCKETS):
                        val = jnp.where(d >= starts[b], tab_ref[b, hh], val)
                    blk = jnp.where(d >= 0, (val - far) * LOG2E, NEG)
                out_ref[0, w, r * sb:(r + 1) * sb, c * sb:(c + 1) * sb] = blk


def _bias_tiles(rel_bias):
    t = ATT_TILE
    return pl.pallas_call(
        functools.partial(_bias_kernel, starts=_bucket_starts()),
        out_shape=jax.ShapeDtypeStruct((HEADS, 2, t, t), F32),
        grid=(HEADS,),
        in_specs=[pl.BlockSpec(memory_space=pltpu.SMEM)],
        out_specs=pl.BlockSpec((1, 2, t, t), lambda i: (i, 0, 0, 0)),
        compiler_params=pltpu.CompilerParams(dimension_semantics=("arbitrary",)),
        name="t5_bias_tiles",
    )(rel_bias)


def _attn_kernel(qt_ref, kk_ref, vt_ref, bias_ref, lam_ref, sw_ref, o_ref,
                 qa_ref, qb_ref, sa_ref, sb_ref, acc_ref, m_ref, *, lambda_init):
    t = ATT_TILE
    nq = kk_ref.shape[1] // t
    n_far = (nq - 1) * (nq - 2) // 2
    unroll = 8

    row = lax.broadcasted_iota(jnp.int32, qt_ref.shape[1:], 0)
    qt = qt_ref[0]
    qa_ref[...] = jnp.where(row < HALF_W, qt, jnp.zeros_like(qt))
    qb_ref[...] = jnp.where(row >= HALF_W, qt, jnp.zeros_like(qt))
    acc_ref[...] = jnp.zeros_like(acc_ref)
    m_ref[...] = jnp.full_like(m_ref, NEG)

    half = t // 2
    whole_block = ((0, t, t),)
    diag_block = ((0, half, half), (half, half, t))

    def scores(qi, ki, dst, pieces=whole_block):
        for mp, q_ref in enumerate((qa_ref, qb_ref)):
            for c0, nc, rows in pieces:
                kt = kk_ref[0, pl.ds(pl.multiple_of(ki * t, t), rows), :]
                qcols = q_ref[:, pl.ds(pl.multiple_of(qi * t + c0, nc), nc)]
                dst[mp, :rows, c0:c0 + nc] = _dot(kt, qcols)

    def accumulate(qi, ki, src, near):
        for mp in range(2):
            for c0, nc, rows in (diag_block if near == 0 else whole_block):
                cs = slice(c0, c0 + nc)
                vt = vt_ref[0, :, pl.ds(pl.multiple_of(ki * t, t), rows)]
                if near == 0:
                    s = src[mp, :rows, cs] + bias_ref[0, 0, :rows, cs]
                elif near == 1:
                    s = jnp.concatenate([src[mp, :t - MAX_DISTANCE],
                                         src[mp, t - MAX_DISTANCE:] + bias_ref[0, 1, t - MAX_DISTANCE:]], axis=0)
                else:
                    s = src[mp]
                m_old = m_ref[qi, mp, :, cs]
                m_new = jnp.maximum(m_old, jnp.max(s, axis=0, keepdims=True))
                p = jnp.exp2(s - m_new).astype(BF16)
                acc_ref[qi, mp, :, cs] = jnp.exp2(m_old - m_new) * acc_ref[qi, mp, :, cs] + _dot(vt, p)
                m_ref[qi, mp, :, cs] = m_new

    lp = lam_ref[...]
    lam = (jnp.exp(jnp.sum(lp[0:1] * lp[1:2], axis=1, keepdims=True))
           - jnp.exp(jnp.sum(lp[2:3] * lp[3:4], axis=1, keepdims=True)) + lambda_init)

    def finalize(qi):
        a1, a2 = acc_ref[qi, 0], acc_ref[qi, 1]
        ot = (a1[:HEAD_W] * (1.0 / a1[HEAD_W:HEAD_W + 1])
              - lam * (a2[:HEAD_W] * (1.0 / a2[HEAD_W:HEAD_W + 1])))
        ms = jnp.mean(ot * ot, axis=0, keepdims=True)
        y = ot * lax.rsqrt(ms + EPS) * sw_ref[...] * (1.0 - lambda_init)
        o_ref[0, pl.ds(pl.multiple_of(qi * t, t), t), :] = y.T.astype(BF16)

    def far_step(c, src, dst):
        qi, ki = c
        wrap = ki >= qi - 2
        last = (qi == nq - 1) & wrap
        qn = jnp.where(last, 0, jnp.where(wrap, qi + 1, qi))
        kn = jnp.where(wrap, 0, ki + 1)
        scores(qn, kn, dst)
        accumulate(qi, ki, src, None)
        return qn, kn

    def near_tiles(qi, count, src, dst):
        for u in range(count):
            q = qi + u
            scores(q, q, dst, diag_block)
            accumulate(q, q - 1, src, 1)
            qn = jnp.minimum(q + 1, nq - 1)
            scores(qn, qn - 1, src)
            accumulate(q, q, dst, 0)
            finalize(q)

    bufs = (sa_ref, sb_ref)
    if n_far > 0:
        scores(2, 0, sa_ref)

        def far_group(_, c):
            for u in range(unroll):
                c = far_step(c, bufs[u % 2], bufs[(u + 1) % 2])
            return c

        c = lax.fori_loop(0, n_far // unroll, far_group, (jnp.int32(2), jnp.int32(0)))
        for u in range(n_far % unroll):
            c = far_step(c, bufs[u % 2], bufs[(u + 1) % 2])
    else:
        scores(0, 0, sa_ref)
    src, dst = bufs[n_far % 2], bufs[(n_far + 1) % 2]
    scores(min(1, nq - 1), 0, dst)
    accumulate(0, 0, src, 0)
    finalize(0)
    per_group = unroll // 2

    def near_group(g, carry):
        near_tiles(1 + g * per_group, per_group, dst, src)
        return carry

    lax.fori_loop(0, (nq - 1) // per_group, near_group, 0)
    rest = (nq - 1) % per_group
    if rest:
        near_tiles(nq - rest, rest, dst, src)


def _attention(qt, kk3, vt, bias, lam_p, sw_col, *, lambda_init):
    b, s, _ = kk3.shape
    t = ATT_TILE
    nq = s // t
    return pl.pallas_call(
        functools.partial(_attn_kernel, lambda_init=lambda_init),
        out_shape=jax.ShapeDtypeStruct((b, s, D_MODEL), BF16),
        grid=(b, HEADS),
        in_specs=[
            pl.BlockSpec((1, HEAD_W, s), lambda i, j: (i, j, 0)),
            pl.BlockSpec((1, s, HEAD_W), lambda i, j: (i, 0, j)),
            pl.BlockSpec((1, V_ROWS, s), lambda i, j: (i, j, 0)),
            pl.BlockSpec((1, 2, t, t), lambda i, j: (j, 0, 0, 0)),
            _resident(lam_p.shape),
            _resident((HEAD_W, 1)),
        ],
        out_specs=pl.BlockSpec((1, s, HEAD_W), lambda i, j: (i, 0, j)),
        scratch_shapes=[pltpu.VMEM((HEAD_W, s), BF16), pltpu.VMEM((HEAD_W, s), BF16),
                        pltpu.VMEM((2, t, t), F32), pltpu.VMEM((2, t, t), F32),
                        pltpu.VMEM((nq, 2, V_ROWS, t), F32),
                        pltpu.VMEM((nq, 2, 1, t), F32)],
        compiler_params=pltpu.CompilerParams(
            dimension_semantics=("arbitrary", "arbitrary"), vmem_limit_bytes=VMEM_LIMIT),
        name="diff_attention",
    )(qt, kk3, vt, bias, lam_p, sw_col)


def _head_major(w):
    d_in = w.shape[0]
    return w.reshape(d_in, 2, HEADS, HALF_W).transpose(0, 2, 1, 3).reshape(d_in, D_MODEL)


def kernel(x, norm_w, ffn_w_in, ffn_w_out, hgrn_w_in, hgrn_lower_bounds, hgrn_gnorm_w, hgrn_w_out,
           kv_norm_w, w_kv, rel_bias, diff_w_q, diff_lambda, diff_subln_w, diff_w_out, final_norm_w):
    b, s, d = x.shape
    assert d == D_MODEL and s % ATT_TILE == 0 and s % FFN_ROWS == 0 and s % (HGRN_TILES * HGRN_ROWS) == 0
    assert ATT_TILE >= MAX_DISTANCE
    n = b * s

    w_in_all, w_out_all = ffn_w_in.astype(BF16), ffn_w_out.astype(BF16)

    def ffn_w(l, j):
        return norm_w[l, 2 * j].reshape(1, d), w_in_all, w_out_all, (l, j)

    wkv = jnp.concatenate([_head_major(w_kv[:, :D_MODEL]), w_kv[:, D_MODEL:]], axis=1).astype(BF16)
    wq = _head_major(diff_w_q[0]).astype(BF16)
    lambda_init = 0.8 - 0.6 * math.exp(-0.3 * 1)

    bias = _bias_tiles(rel_bias)

    h = _ffn(x.reshape(n, d), *ffn_w(0, 0), batch=b)
    h = _hgrn(h.reshape(b, s, d), norm_w[0, 1].reshape(1, d), hgrn_w_in[0].astype(BF16),
              hgrn_lower_bounds, hgrn_gnorm_w[0].reshape(1, HEAD_W), hgrn_w_out[0].astype(BF16))
    h, kk, vt = _ffn(h.reshape(n, d), *ffn_w(0, 1), batch=b, epilogue="kv",
                     epi_args=(kv_norm_w.reshape(1, d), wkv))
    h, qt = _ffn(h, *ffn_w(1, 0), batch=b, epilogue="q", epi_args=(norm_w[1, 1].reshape(1, d), wq))
    o = _attention(qt, kk.reshape(b, s, d), vt, bias, diff_lambda[0],
                   diff_subln_w[0].reshape(HEAD_W, 1), lambda_init=lambda_init)
    out = _ffn(h, *ffn_w(1, 1), batch=b, prologue="attn_out",
               pro_args=(o.reshape(n, d), diff_w_out[0].astype(BF16)),
               epilogue="final", epi_args=(final_norm_w.reshape(1, d),))
    return out.reshape(b, s, d)
```

```python
import functools
import math

import numpy as np
import jax
import jax.numpy as jnp
from jax import lax
from jax.experimental import pallas as pl
from jax.experimental.pallas import tpu as pltpu

F32 = jnp.float32
BF16 = jnp.bfloat16

D_MODEL = 1024
D_FF = 2816
EPS = 1e-6
FFN_RES = 0.5
HEADS = 8
HEAD_W = D_MODEL // HEADS
HALF_W = HEAD_W // 2
HGRN_CHUNK = 64
N_BUCKETS = 32
MAX_DISTANCE = 128
LOG2E = math.log2(math.e)
NEG = -1e30

FFN_ROWS = 1024
FFN_COLS = 256
HGRN_ROWS = 256
HGRN_TILES = 4
ATT_TILE = 512
V_ROWS = HEAD_W + 16
VMEM_V7X = 64 * 1024 * 1024
VMEM_LIMIT = VMEM_V7X * 7 // 8

NT_DIMS = (((1,), (1,)), ((), ()))
TN_DIMS = (((0,), (0,)), ((), ()))


def _dot(a, b):
    return jnp.dot(a, b, preferred_element_type=F32)


def _rms(x, w):
    return x * lax.rsqrt(jnp.mean(x * x, axis=-1, keepdims=True) + EPS) * w


def _silu(x):
    return x * jax.nn.sigmoid(x)


def _resident(shape):
    zeros = (0,) * len(shape)
    return pl.BlockSpec(shape, lambda *_: zeros, pipeline_mode=pl.Buffered(1))


def _ffn_kernel(*refs, prologue, epilogue):
    it = iter(refs)
    h_ref = next(it)
    if prologue == "attn_out":
        o_ref, wo_ref = next(it), next(it)
    nw_ref, win_ref, wd_ref = next(it), next(it), next(it)
    if epilogue is not None:
        enw_ref = next(it)
    if epilogue in ("kv", "q"):
        ew_ref = next(it)
    outs = [next(it) for _ in range({None: 1, "final": 1, "q": 3, "kv": 3}[epilogue])]
    act_ref = next(it)

    h = h_ref[...]
    if prologue == "attn_out":
        h = h + _dot(o_ref[...], wo_ref[...])
    xn = _rms(h, nw_ref[...]).astype(BF16)
    for c0 in range(0, D_FF, FFN_COLS):
        c1 = min(c0 + FFN_COLS, D_FF)
        g = _dot(xn, win_ref[:, c0:c1])
        u = _dot(xn, win_ref[:, D_FF + c0:D_FF + c1])
        act_ref[:, c0:c1] = (_silu(g) * u).astype(BF16)
    h = h + FFN_RES * _dot(act_ref[...], wd_ref[...])

    if epilogue == "final":
        outs[0][...] = _rms(h, enw_ref[...])
        return
    outs[0][...] = h
    if epilogue == "kv":
        kv = _dot(_rms(h, enw_ref[...]).astype(BF16), ew_ref[...])
        outs[1][...] = kv[:, :D_MODEL].astype(BF16)
        vt = kv[:, D_MODEL:].T.astype(BF16)
        for hh in range(HEADS):
            outs[2][0, hh * V_ROWS:hh * V_ROWS + HEAD_W, :] = vt[hh * HEAD_W:(hh + 1) * HEAD_W]
            outs[2][0, hh * V_ROWS + HEAD_W:(hh + 1) * V_ROWS, :] = jnp.ones((V_ROWS - HEAD_W, vt.shape[1]), BF16)
    elif epilogue == "q":
        q = _dot(_rms(h, enw_ref[...]).astype(BF16), ew_ref[...])
        qt = (q * (HALF_W ** -0.5 * LOG2E)).T.astype(BF16)
        first_half = lax.broadcasted_iota(jnp.int32, qt.shape, 0) % HEAD_W < HALF_W
        outs[1][0] = jnp.where(first_half, qt, jnp.zeros_like(qt))
        outs[2][0] = jnp.where(first_half, jnp.zeros_like(qt), qt)


def _ffn(h, nw, win_all, wd_all, which, *, batch, prologue=None, pro_args=(), epilogue=None, epi_args=()):
    n = h.shape[0]
    seq = n // batch
    tm = FFN_ROWS
    per_seq = seq // tm
    row = pl.BlockSpec((tm, D_MODEL), lambda i: (i, 0))
    tr = pl.BlockSpec((1, D_MODEL, tm), lambda i: (i // per_seq, 0, i % per_seq))
    in_specs, args = [row], [h]
    if prologue == "attn_out":
        in_specs += [row, _resident((D_MODEL, D_MODEL))]
        args += list(pro_args)
    in_specs += [_resident((1, D_MODEL)),
                 pl.BlockSpec((None, None, D_MODEL, 2 * D_FF), lambda i: (*which, 0, 0), pipeline_mode=pl.Buffered(1)),
                 pl.BlockSpec((None, None, D_FF, D_MODEL), lambda i: (*which, 0, 0), pipeline_mode=pl.Buffered(1))]
    args += [nw, win_all, wd_all]
    if epilogue is not None:
        in_specs.append(_resident((1, D_MODEL)))
    if epilogue in ("kv", "q"):
        in_specs.append(_resident(epi_args[1].shape))
    args += list(epi_args)
    h_shape = jax.ShapeDtypeStruct((n, D_MODEL), F32)
    t_shape = jax.ShapeDtypeStruct((batch, D_MODEL, seq), BF16)
    if epilogue in (None, "final"):
        out_shape, out_specs = h_shape, row
    elif epilogue == "q":
        out_shape, out_specs = (h_shape, t_shape, t_shape), (row, tr, tr)
    else:
        out_shape = (h_shape, jax.ShapeDtypeStruct((n, D_MODEL), BF16),
                     jax.ShapeDtypeStruct((batch, HEADS * V_ROWS, seq), BF16))
        out_specs = (row, row, pl.BlockSpec((1, HEADS * V_ROWS, tm), tr.index_map))
    return pl.pallas_call(
        functools.partial(_ffn_kernel, prologue=prologue, epilogue=epilogue),
        out_shape=out_shape,
        grid=(n // tm,),
        in_specs=in_specs,
        out_specs=out_specs,
        scratch_shapes=[pltpu.VMEM((tm, D_FF), BF16)],
        compiler_params=pltpu.CompilerParams(
            dimension_semantics=("arbitrary",), vmem_limit_bytes=VMEM_LIMIT),
        name="ffn_" + str(prologue) + "_" + str(epilogue),
    )(*args)


def _hgrn_kernel(h_ref, nw_ref, win_ref, lbraw_ref, ltri_ref, gw_ref, wout_ref, out_ref,
                 z_ref, st_ref, o_ref, og_ref):
    t = HGRN_ROWS
    n_buf = z_ref.shape[0]

    @pl.when(pl.program_id(1) == 0)
    def _():
        st_ref[...] = jnp.zeros_like(st_ref)

    def project(i):
        h = h_ref[0, i * t:(i + 1) * t]
        z_ref[i % n_buf] = _dot(_rms(h, nw_ref[...]).astype(BF16), win_ref[...])

    project(0)
    for i in range(HGRN_TILES):
        if i + 1 < HGRN_TILES:
            project(i + 1)
        out_ref[0, i * t:(i + 1) * t] = _hgrn_mix(
            h_ref[0, i * t:(i + 1) * t], z_ref.at[i % n_buf], o_ref.at[i % 2], og_ref.at[i % 2],
            lbraw_ref=lbraw_ref, ltri_ref=ltri_ref, gw_ref=gw_ref, wout_ref=wout_ref, st_ref=st_ref)


def _hgrn_mix(h, z_ref, o_ref, og_ref, *, lbraw_ref, ltri_ref, gw_ref, wout_ref, st_ref):
    t = h.shape[0]
    c = HGRN_CHUNK
    q = _silu(z_ref[:, :D_MODEL])
    lbr = lbraw_ref[...]
    e = jnp.exp(lbr - jnp.max(lbr, axis=0, keepdims=True))
    lb = e[0:1] / jnp.sum(e, axis=0, keepdims=True)
    f = lb + (1.0 - lb) * jax.nn.sigmoid(z_ref[:, D_MODEL:2 * D_MODEL])
    k = 1.0 - f
    logf = jnp.log(f)
    hi = logf.astype(BF16)
    lo = (logf - hi.astype(F32)).astype(BF16)
    ltri = ltri_ref[...]
    bc = _dot(ltri, hi) + _dot(ltri, lo)
    v = z_ref[:, 2 * D_MODEL:3 * D_MODEL]

    tril = lax.broadcasted_iota(jnp.int32, (c, c), 0) >= lax.broadcasted_iota(jnp.int32, (c, c), 1)
    for n in range(t // c):
        r = slice(n * c, (n + 1) * c)
        bcn = bc[r]
        b_last = bcn[c - 1:c]
        b_mid = bcn[c // 2 - 1:c // 2]
        qn, kn = q[r], k[r]
        qd = (qn * jnp.exp(bcn)).astype(BF16)
        kd = (kn * jnp.exp(b_last - bcn)).astype(BF16)
        qi = (qn * jnp.exp(bcn - b_mid)).astype(BF16)
        ki = (kn * jnp.exp(b_mid - bcn)).astype(BF16)
        dec = jnp.exp(b_last)
        vb = v[r].astype(BF16)
        for hh in range(HEADS):
            cs = slice(hh * HEAD_W, (hh + 1) * HEAD_W)
            a = lax.dot_general(qi[:, cs], ki[:, cs], NT_DIMS, preferred_element_type=F32)
            a = jnp.where(tril, a, 0.0).astype(BF16)
            st = st_ref[hh]
            o_ref[r, cs] = _dot(jnp.concatenate([qd[:, cs], a], axis=1),
                                jnp.concatenate([st.T.astype(BF16), vb[:, cs]], axis=0))
            ut = lax.dot_general(vb[:, cs], kd[:, cs], TN_DIMS, preferred_element_type=F32)
            st_ref[hh] = st * dec[:, cs] + ut

    gate = _silu(z_ref[:, 3 * D_MODEL:])
    gw = gw_ref[...]
    for hh in range(HEADS):
        cs = slice(hh * HEAD_W, (hh + 1) * HEAD_W)
        og_ref[:, cs] = (_rms(o_ref[:, cs], gw) * gate[:, cs]).astype(BF16)
    return h + _dot(og_ref[...], wout_ref[...])


def _hgrn(h3, nw, win, lbraw, gw, wout):
    b, s, _ = h3.shape
    t = HGRN_ROWS
    rows = HGRN_TILES * t
    idx = np.arange(t)
    ltri = jnp.asarray((idx[:, None] // HGRN_CHUNK == idx[None, :] // HGRN_CHUNK)
                       & (idx[None, :] <= idx[:, None]), BF16)
    blk = pl.BlockSpec((1, rows, D_MODEL), lambda i, j: (i, j, 0))
    return pl.pallas_call(
        _hgrn_kernel,
        out_shape=jax.ShapeDtypeStruct(h3.shape, F32),
        grid=(b, s // rows),
        in_specs=[blk, _resident((1, D_MODEL)), _resident(win.shape), _resident(lbraw.shape),
                  _resident((t, t)), _resident((1, HEAD_W)), _resident((D_MODEL, D_MODEL))],
        out_specs=blk,
        scratch_shapes=[pltpu.VMEM((min(3, HGRN_TILES), t, 4 * D_MODEL), F32),
                        pltpu.VMEM((HEADS, HEAD_W, HEAD_W), F32),
                        pltpu.VMEM((2, t, D_MODEL), F32),
                        pltpu.VMEM((2, t, D_MODEL), BF16)],
        compiler_params=pltpu.CompilerParams(
            dimension_semantics=("arbitrary", "arbitrary"), vmem_limit_bytes=VMEM_LIMIT),
        name="hgrn_layer",
    )(h3, nw, win, lbraw, ltri, gw, wout)


def _bucket_starts():
    n = np.arange(4 * MAX_DISTANCE, dtype=np.int64)
    max_exact = N_BUCKETS // 2
    nf = np.maximum(n, 1).astype(np.float32)
    large = max_exact + (np.log(nf / np.float32(max_exact)) / np.float32(math.log(MAX_DISTANCE / max_exact))
                         * np.float32(N_BUCKETS - max_exact)).astype(np.int32)
    bucket = np.where(n < max_exact, n, np.minimum(large, N_BUCKETS - 1))
    assert np.all(np.diff(bucket) >= 0) and bucket[-1] == N_BUCKETS - 1
    return [int(np.argmax(bucket >= b)) for b in range(N_BUCKETS)]


def _bias_kernel(tab_ref, out_ref, *, starts):
    hh = pl.program_id(0)
    t = out_ref.shape[2]
    sb = MAX_DISTANCE
    far = tab_ref[N_BUCKETS - 1, hh]
    in_block = lax.broadcasted_iota(jnp.int32, (sb, sb), 1) - lax.broadcasted_iota(jnp.int32, (sb, sb), 0)
    for w in range(2):
        for r in range(t // sb):
            for c in range(t // sb):
                off = w * t + sb * (c - r)
                if off + (sb - 1) < 0:
                    blk = jnp.full((sb, sb), NEG, F32)
                elif off - (sb - 1) >= starts[N_BUCKETS - 1]:
                    blk = jnp.zeros((sb, sb), F32)
                else:
                    d = in_block + off
                    val = jnp.full((sb, sb), tab_ref[0, hh], F32)
                    for b in range(1, N_BUCKETS):
                        val = jnp.where(d >= starts[b], tab_ref[b, hh], val)
                    blk = jnp.where(d >= 0, (val - far) * LOG2E, NEG)
                out_ref[0, w, r * sb:(r + 1) * sb, c * sb:(c + 1) * sb] = blk


def _bias_tiles(rel_bias):
    t = ATT_TILE
    return pl.pallas_call(
        functools.partial(_bias_kernel, starts=_bucket_starts()),
        out_shape=jax.ShapeDtypeStruct((HEADS, 2, t, t), F32),
        grid=(HEADS,),
        in_specs=[pl.BlockSpec(memory_space=pltpu.SMEM)],
        out_specs=pl.BlockSpec((1, 2, t, t), lambda i: (i, 0, 0, 0)),
        compiler_params=pltpu.CompilerParams(dimension_semantics=("arbitrary",)),
        name="t5_bias_tiles",
    )(rel_bias)


def _attn_kernel(qa_ref, qb_ref, kk_ref, vt_ref, bias_ref, lam_ref, sw_ref, o_ref,
                 sa_ref, sb_ref, acc_ref, m_ref, *, lambda_init):
    t = ATT_TILE
    nq = kk_ref.shape[1] // t
    n_far = (nq - 1) * (nq - 2) // 2
    unroll = 8

    acc_ref[...] = jnp.zeros_like(acc_ref)
    m_ref[...] = jnp.full_like(m_ref, NEG)

    half = t // 2
    whole_block = ((0, t, t),)
    diag_block = ((0, half, half), (half, half, t))

    def scores(qi, ki, dst, pieces=whole_block):
        for mp, q_ref in enumerate((qa_ref, qb_ref)):
            for c0, nc, rows in pieces:
                kt = kk_ref[0, pl.ds(pl.multiple_of(ki * t, t), rows), :]
                qcols = q_ref[0, :, pl.ds(pl.multiple_of(qi * t + c0, nc), nc)]
                dst[mp, :rows, c0:c0 + nc] = _dot(kt, qcols)

    def accumulate(qi, ki, src, near):
        for mp in range(2):
            for c0, nc, rows in (diag_block if near == 0 else whole_block):
                cs = slice(c0, c0 + nc)
                vt = vt_ref[0, :, pl.ds(pl.multiple_of(ki * t, t), rows)]
                if near == 0:
                    s = src[mp, :rows, cs] + bias_ref[0, 0, :rows, cs]
                elif near == 1:
                    s = jnp.concatenate([src[mp, :t - MAX_DISTANCE],
                                         src[mp, t - MAX_DISTANCE:] + bias_ref[0, 1, t - MAX_DISTANCE:]], axis=0)
                else:
                    s = src[mp]
                m_old = m_ref[qi, mp, :, cs]
                m_new = jnp.maximum(m_old, jnp.max(s, axis=0, keepdims=True))
                p = jnp.exp2(s - m_new).astype(BF16)
                acc_ref[qi, mp, :, cs] = jnp.exp2(m_old - m_new) * acc_ref[qi, mp, :, cs] + _dot(vt, p)
                m_ref[qi, mp, :, cs] = m_new

    lp = lam_ref[...]
    lam = (jnp.exp(jnp.sum(lp[0:1] * lp[1:2], axis=1, keepdims=True))
           - jnp.exp(jnp.sum(lp[2:3] * lp[3:4], axis=1, keepdims=True)) + lambda_init)

    def finalize(qi):
        a1, a2 = acc_ref[qi, 0], acc_ref[qi, 1]
        ot = (a1[:HEAD_W] * (1.0 / a1[HEAD_W:HEAD_W + 1])
              - lam * (a2[:HEAD_W] * (1.0 / a2[HEAD_W:HEAD_W + 1])))
        ms = jnp.mean(ot * ot, axis=0, keepdims=True)
        y = ot * lax.rsqrt(ms + EPS) * sw_ref[...] * (1.0 - lambda_init)
        o_ref[0, pl.ds(pl.multiple_of(qi * t, t), t), :] = y.T.astype(BF16)

    def far_step(c, src, dst):
        qi, ki = c
        wrap = ki >= qi - 2
        last = (qi == nq - 1) & wrap
        qn = jnp.where(last, 0, jnp.where(wrap, qi + 1, qi))
        kn = jnp.where(wrap, 0, ki + 1)
        scores(qn, kn, dst)
        accumulate(qi, ki, src, None)
        return qn, kn

    def near_tiles(qi, count, src, dst):
        for u in range(count):
            q = qi + u
            scores(q, q, dst, diag_block)
            accumulate(q, q - 1, src, 1)
            qn = jnp.minimum(q + 1, nq - 1)
            scores(qn, qn - 1, src)
            accumulate(q, q, dst, 0)
            finalize(q)

    bufs = (sa_ref, sb_ref)
    if n_far > 0:
        scores(2, 0, sa_ref)

        def far_group(_, c):
            for u in range(unroll):
                c = far_step(c, bufs[u % 2], bufs[(u + 1) % 2])
            return c

        c = lax.fori_loop(0, n_far // unroll, far_group, (jnp.int32(2), jnp.int32(0)))
        for u in range(n_far % unroll):
            c = far_step(c, bufs[u % 2], bufs[(u + 1) % 2])
    else:
        scores(0, 0, sa_ref)
    src, dst = bufs[n_far % 2], bufs[(n_far + 1) % 2]
    scores(min(1, nq - 1), 0, dst)
    accumulate(0, 0, src, 0)
    finalize(0)
    per_group = unroll // 2

    def near_group(g, carry):
        near_tiles(1 + g * per_group, per_group, dst, src)
        return carry

    lax.fori_loop(0, (nq - 1) // per_group, near_group, 0)
    rest = (nq - 1) % per_group
    if rest:
        near_tiles(nq - rest, rest, dst, src)


def _attention(qa, qb, kk3, vt, bias, lam_p, sw_col, *, lambda_init):
    b, s, _ = kk3.shape
    t = ATT_TILE
    nq = s // t
    return pl.pallas_call(
        functools.partial(_attn_kernel, lambda_init=lambda_init),
        out_shape=jax.ShapeDtypeStruct((b, s, D_MODEL), BF16),
        grid=(b, HEADS),
        in_specs=[
            pl.BlockSpec((1, HEAD_W, s), lambda i, j: (i, j, 0)),
            pl.BlockSpec((1, HEAD_W, s), lambda i, j: (i, j, 0)),
            pl.BlockSpec((1, s, HEAD_W), lambda i, j: (i, 0, j)),
            pl.BlockSpec((1, V_ROWS, s), lambda i, j: (i, j, 0)),
            pl.BlockSpec((1, 2, t, t), lambda i, j: (j, 0, 0, 0)),
            _resident(lam_p.shape),
            _resident((HEAD_W, 1)),
        ],
        out_specs=pl.BlockSpec((1, s, HEAD_W), lambda i, j: (i, 0, j)),
        scratch_shapes=[pltpu.VMEM((2, t, t), F32), pltpu.VMEM((2, t, t), F32),
                        pltpu.VMEM((nq, 2, V_ROWS, t), F32),
                        pltpu.VMEM((nq, 2, 1, t), F32)],
        compiler_params=pltpu.CompilerParams(
            dimension_semantics=("arbitrary", "arbitrary"), vmem_limit_bytes=VMEM_LIMIT),
        name="diff_attention",
    )(qa, qb, kk3, vt, bias, lam_p, sw_col)


def _head_major(w):
    d_in = w.shape[0]
    return w.reshape(d_in, 2, HEADS, HALF_W).transpose(0, 2, 1, 3).reshape(d_in, D_MODEL)


def kernel(x, norm_w, ffn_w_in, ffn_w_out, hgrn_w_in, hgrn_lower_bounds, hgrn_gnorm_w, hgrn_w_out,
           kv_norm_w, w_kv, rel_bias, diff_w_q, diff_lambda, diff_subln_w, diff_w_out, final_norm_w):
    b, s, d = x.shape
    assert d == D_MODEL and s % ATT_TILE == 0 and s % FFN_ROWS == 0 and s % (HGRN_TILES * HGRN_ROWS) == 0
    assert ATT_TILE >= MAX_DISTANCE
    n = b * s

    w_in_all, w_out_all = ffn_w_in.astype(BF16), ffn_w_out.astype(BF16)

    def ffn_w(l, j):
        return norm_w[l, 2 * j].reshape(1, d), w_in_all, w_out_all, (l, j)

    wkv = jnp.concatenate([_head_major(w_kv[:, :D_MODEL]), w_kv[:, D_MODEL:]], axis=1).astype(BF16)
    wq = _head_major(diff_w_q[0]).astype(BF16)
    lambda_init = 0.8 - 0.6 * math.exp(-0.3 * 1)

    bias = _bias_tiles(rel_bias)

    h = _ffn(x.reshape(n, d), *ffn_w(0, 0), batch=b)
    h = _hgrn(h.reshape(b, s, d), norm_w[0, 1].reshape(1, d), hgrn_w_in[0].astype(BF16),
              hgrn_lower_bounds, hgrn_gnorm_w[0].reshape(1, HEAD_W), hgrn_w_out[0].astype(BF16))
    h, kk, vt = _ffn(h.reshape(n, d), *ffn_w(0, 1), batch=b, epilogue="kv",
                     epi_args=(kv_norm_w.reshape(1, d), wkv))
    h, qa, qb = _ffn(h, *ffn_w(1, 0), batch=b, epilogue="q", epi_args=(norm_w[1, 1].reshape(1, d), wq))
    o = _attention(qa, qb, kk.reshape(b, s, d), vt, bias, diff_lambda[0],
                   diff_subln_w[0].reshape(HEAD_W, 1), lambda_init=lambda_init)
    out = _ffn(h, *ffn_w(1, 1), batch=b, prologue="attn_out",
               pro_args=(o.reshape(n, d), diff_w_out[0].astype(BF16)),
               epilogue="final", epi_args=(final_norm_w.reshape(1, d),))
    return out.reshape(b, s, d)
```

```python
import functools
import math

import numpy as np
import jax
import jax.numpy as jnp
from jax import lax
from jax.experimental import pallas as pl
from jax.experimental.pallas import tpu as pltpu

F32 = jnp.float32
BF16 = jnp.bfloat16

D_MODEL = 1024
D_FF = 2816
EPS = 1e-6
FFN_RES = 0.5
HEADS = 8
HEAD_W = D_MODEL // HEADS
HALF_W = HEAD_W // 2
HGRN_CHUNK = 64
N_BUCKETS = 32
MAX_DISTANCE = 128
LOG2E = math.log2(math.e)
NEG = -1e30

FFN_ROWS = 1024
FFN_COLS = 256
HGRN_ROWS = 256
HGRN_TILES = 4
ATT_TILE = 512
V_ROWS = HEAD_W + 16
VMEM_LIMIT = 56 * 1024 * 1024

NT_DIMS = (((1,), (1,)), ((), ()))
TN_DIMS = (((0,), (0,)), ((), ()))


def _dot(a, b):
    return jnp.dot(a, b, preferred_element_type=F32)


def _rms(x, w):
    return x * lax.rsqrt(jnp.mean(x * x, axis=-1, keepdims=True) + EPS) * w


def _silu(x):
    return x * jax.nn.sigmoid(x)


def _resident(shape):
    zeros = (0,) * len(shape)
    return pl.BlockSpec(shape, lambda *_: zeros, pipeline_mode=pl.Buffered(1))


def _ffn_kernel(*refs, prologue, epilogue):
    it = iter(refs)
    h_ref = next(it)
    if prologue == "attn_out":
        o_ref, wo_ref = next(it), next(it)
    nw_ref, win_ref, wd_ref = next(it), next(it), next(it)
    if epilogue is not None:
        enw_ref = next(it)
    if epilogue in ("kv", "q"):
        ew_ref = next(it)
    outs = [next(it) for _ in range({None: 1, "final": 1, "q": 2, "kv": 3}[epilogue])]
    act_ref = next(it)

    h = h_ref[...]
    if prologue == "attn_out":
        h = h + _dot(o_ref[...], wo_ref[...])
    xn = _rms(h, nw_ref[...]).astype(BF16)
    for c in range(D_FF // FFN_COLS):
        sl = slice(c * FFN_COLS, (c + 1) * FFN_COLS)
        g = _dot(xn, win_ref[:, sl])
        u = _dot(xn, win_ref[:, D_FF + c * FFN_COLS:D_FF + (c + 1) * FFN_COLS])
        act_ref[:, sl] = (_silu(g) * u).astype(BF16)
    h = h + FFN_RES * _dot(act_ref[...], wd_ref[...])

    if epilogue == "final":
        outs[0][...] = _rms(h, enw_ref[...])
        return
    outs[0][...] = h
    if epilogue == "kv":
        kv = _dot(_rms(h, enw_ref[...]).astype(BF16), ew_ref[...])
        outs[1][...] = kv[:, :D_MODEL].astype(BF16)
        vt = kv[:, D_MODEL:].T.astype(BF16)
        for hh in range(HEADS):
            outs[2][0, hh * V_ROWS:hh * V_ROWS + HEAD_W, :] = vt[hh * HEAD_W:(hh + 1) * HEAD_W]
            outs[2][0, hh * V_ROWS + HEAD_W:(hh + 1) * V_ROWS, :] = jnp.ones((V_ROWS - HEAD_W, vt.shape[1]), BF16)
    elif epilogue == "q":
        q = _dot(_rms(h, enw_ref[...]).astype(BF16), ew_ref[...])
        outs[1][0] = (q * (HALF_W ** -0.5 * LOG2E)).T.astype(BF16)


def _ffn(h, nw, win_all, wd_all, which, *, batch, prologue=None, pro_args=(), epilogue=None, epi_args=()):
    n = h.shape[0]
    seq = n // batch
    tm = FFN_ROWS
    per_seq = seq // tm
    row = pl.BlockSpec((tm, D_MODEL), lambda i: (i, 0))
    tr = pl.BlockSpec((1, D_MODEL, tm), lambda i: (i // per_seq, 0, i % per_seq))
    in_specs, args = [row], [h]
    if prologue == "attn_out":
        in_specs += [row, _resident((D_MODEL, D_MODEL))]
        args += list(pro_args)
    in_specs += [_resident((1, D_MODEL)),
                 pl.BlockSpec((None, None, D_MODEL, 2 * D_FF), lambda i: (*which, 0, 0), pipeline_mode=pl.Buffered(1)),
                 pl.BlockSpec((None, None, D_FF, D_MODEL), lambda i: (*which, 0, 0), pipeline_mode=pl.Buffered(1))]
    args += [nw, win_all, wd_all]
    if epilogue is not None:
        in_specs.append(_resident((1, D_MODEL)))
    if epilogue in ("kv", "q"):
        in_specs.append(_resident(epi_args[1].shape))
    args += list(epi_args)
    h_shape = jax.ShapeDtypeStruct((n, D_MODEL), F32)
    t_shape = jax.ShapeDtypeStruct((batch, D_MODEL, seq), BF16)
    if epilogue in (None, "final"):
        out_shape, out_specs = h_shape, row
    elif epilogue == "q":
        out_shape, out_specs = (h_shape, t_shape), (row, tr)
    else:
        out_shape = (h_shape, jax.ShapeDtypeStruct((n, D_MODEL), BF16),
                     jax.ShapeDtypeStruct((batch, HEADS * V_ROWS, seq), BF16))
        out_specs = (row, row, pl.BlockSpec((1, HEADS * V_ROWS, tm), tr.index_map))
    return pl.pallas_call(
        functools.partial(_ffn_kernel, prologue=prologue, epilogue=epilogue),
        out_shape=out_shape,
        grid=(n // tm,),
        in_specs=in_specs,
        out_specs=out_specs,
        scratch_shapes=[pltpu.VMEM((tm, D_FF), BF16)],
        compiler_params=pltpu.CompilerParams(
            dimension_semantics=("arbitrary",), vmem_limit_bytes=VMEM_LIMIT),
        name="ffn_" + str(prologue) + "_" + str(epilogue),
    )(*args)


def _hgrn_kernel(h_ref, nw_ref, win_ref, lbraw_ref, ltri_ref, gw_ref, wout_ref, out_ref,
                 z_ref, st_ref, o_ref, og_ref):
    t = HGRN_ROWS
    n_buf = z_ref.shape[0]

    @pl.when(pl.program_id(1) == 0)
    def _():
        st_ref[...] = jnp.zeros_like(st_ref)

    def project(i):
        h = h_ref[0, i * t:(i + 1) * t]
        z_ref[i % n_buf] = _dot(_rms(h, nw_ref[...]).astype(BF16), win_ref[...])

    project(0)
    for i in range(HGRN_TILES):
        if i + 1 < HGRN_TILES:
            project(i + 1)
        out_ref[0, i * t:(i + 1) * t] = _hgrn_mix(
            h_ref[0, i * t:(i + 1) * t], z_ref.at[i % n_buf], o_ref.at[i % 2], og_ref.at[i % 2],
            lbraw_ref=lbraw_ref, ltri_ref=ltri_ref, gw_ref=gw_ref, wout_ref=wout_ref, st_ref=st_ref)


def _hgrn_mix(h, z_ref, o_ref, og_ref, *, lbraw_ref, ltri_ref, gw_ref, wout_ref, st_ref):
    t = h.shape[0]
    c = HGRN_CHUNK
    q = _silu(z_ref[:, :D_MODEL])
    lbr = lbraw_ref[...]
    e = jnp.exp(lbr - jnp.max(lbr, axis=0, keepdims=True))
    lb = e[0:1] / jnp.sum(e, axis=0, keepdims=True)
    f = lb + (1.0 - lb) * jax.nn.sigmoid(z_ref[:, D_MODEL:2 * D_MODEL])
    k = 1.0 - f
    logf = jnp.log(f)
    hi = logf.astype(BF16)
    lo = (logf - hi.astype(F32)).astype(BF16)
    ltri = ltri_ref[...]
    bc = _dot(ltri, hi) + _dot(ltri, lo)
    v = z_ref[:, 2 * D_MODEL:3 * D_MODEL]

    tril = lax.broadcasted_iota(jnp.int32, (c, c), 0) >= lax.broadcasted_iota(jnp.int32, (c, c), 1)
    for n in range(t // c):
        r = slice(n * c, (n + 1) * c)
        bcn = bc[r]
        b_last = bcn[c - 1:c]
        b_mid = bcn[c // 2 - 1:c // 2]
        qn, kn = q[r], k[r]
        qd = (qn * jnp.exp(bcn)).astype(BF16)
        kd = (kn * jnp.exp(b_last - bcn)).astype(BF16)
        qi = (qn * jnp.exp(bcn - b_mid)).astype(BF16)
        ki = (kn * jnp.exp(b_mid - bcn)).astype(BF16)
        dec = jnp.exp(b_last)
        vb = v[r].astype(BF16)
        for hh in range(HEADS):
            cs = slice(hh * HEAD_W, (hh + 1) * HEAD_W)
            a = lax.dot_general(qi[:, cs], ki[:, cs], NT_DIMS, preferred_element_type=F32)
            a = jnp.where(tril, a, 0.0).astype(BF16)
            st = st_ref[hh]
            o_ref[r, cs] = _dot(jnp.concatenate([qd[:, cs], a], axis=1),
                                jnp.concatenate([st.T.astype(BF16), vb[:, cs]], axis=0))
            ut = lax.dot_general(vb[:, cs], kd[:, cs], TN_DIMS, preferred_element_type=F32)
            st_ref[hh] = st * dec[:, cs] + ut

    gate = _silu(z_ref[:, 3 * D_MODEL:])
    gw = gw_ref[...]
    for hh in range(HEADS):
        cs = slice(hh * HEAD_W, (hh + 1) * HEAD_W)
        og_ref[:, cs] = (_rms(o_ref[:, cs], gw) * gate[:, cs]).astype(BF16)
    return h + _dot(og_ref[...], wout_ref[...])


def _hgrn(h3, nw, win, lbraw, gw, wout):
    b, s, _ = h3.shape
    t = HGRN_ROWS
    rows = HGRN_TILES * t
    idx = np.arange(t)
    ltri = jnp.asarray((idx[:, None] // HGRN_CHUNK == idx[None, :] // HGRN_CHUNK)
                       & (idx[None, :] <= idx[:, None]), BF16)
    blk = pl.BlockSpec((1, rows, D_MODEL), lambda i, j: (i, j, 0))
    return pl.pallas_call(
        _hgrn_kernel,
        out_shape=jax.ShapeDtypeStruct(h3.shape, F32),
        grid=(b, s // rows),
        in_specs=[blk, _resident((1, D_MODEL)), _resident(win.shape), _resident(lbraw.shape),
                  _resident((t, t)), _resident((1, HEAD_W)), _resident((D_MODEL, D_MODEL))],
        out_specs=blk,
        scratch_shapes=[pltpu.VMEM((min(3, HGRN_TILES), t, 4 * D_MODEL), F32),
                        pltpu.VMEM((HEADS, HEAD_W, HEAD_W), F32),
                        pltpu.VMEM((2, t, D_MODEL), F32),
                        pltpu.VMEM((2, t, D_MODEL), BF16)],
        compiler_params=pltpu.CompilerParams(
            dimension_semantics=("arbitrary", "arbitrary"), vmem_limit_bytes=VMEM_LIMIT),
        name="hgrn_layer",
    )(h3, nw, win, lbraw, ltri, gw, wout)


def _bucket_starts():
    n = np.arange(4 * MAX_DISTANCE, dtype=np.int64)
    max_exact = N_BUCKETS // 2
    nf = np.maximum(n, 1).astype(np.float32)
    large = max_exact + (np.log(nf / np.float32(max_exact)) / np.float32(math.log(MAX_DISTANCE / max_exact))
                         * np.float32(N_BUCKETS - max_exact)).astype(np.int32)
    bucket = np.where(n < max_exact, n, np.minimum(large, N_BUCKETS - 1))
    assert np.all(np.diff(bucket) >= 0) and bucket[-1] == N_BUCKETS - 1
    return [int(np.argmax(bucket >= b)) for b in range(N_BUCKETS)]


def _bias_kernel(tab_ref, out_ref, *, starts):
    hh = pl.program_id(0)
    t = out_ref.shape[2]
    sb = MAX_DISTANCE
    far = tab_ref[N_BUCKETS - 1, hh]
    in_block = lax.broadcasted_iota(jnp.int32, (sb, sb), 1) - lax.broadcasted_iota(jnp.int32, (sb, sb), 0)
    for w in range(2):
        for r in range(t // sb):
            for c in range(t // sb):
                off = w * t + sb * (c - r)
                if off + (sb - 1) < 0:
                    blk = jnp.full((sb, sb), NEG, F32)
                elif off - (sb - 1) >= starts[N_BUCKETS - 1]:
                    blk = jnp.zeros((sb, sb), F32)
                else:
                    d = in_block + off
                    val = jnp.full((sb, sb), tab_ref[0, hh], F32)
                    for b in range(1, N_BUCKETS):
                        val = jnp.where(d >= starts[b], tab_ref[b, hh], val)
                    blk = jnp.where(d >= 0, (val - far) * LOG2E, NEG)
                out_ref[0, w, r * sb:(r + 1) * sb, c * sb:(c + 1) * sb] = blk


def _bias_tiles(rel_bias):
    t = ATT_TILE
    return pl.pallas_call(
        functools.partial(_bias_kernel, starts=_bucket_starts()),
        out_shape=jax.ShapeDtypeStruct((HEADS, 2, t, t), F32),
        grid=(HEADS,),
        in_specs=[pl.BlockSpec(memory_space=pltpu.SMEM)],
        out_specs=pl.BlockSpec((1, 2, t, t), lambda i: (i, 0, 0, 0)),
        compiler_params=pltpu.CompilerParams(dimension_semantics=("arbitrary",)),
        name="t5_bias_tiles",
    )(rel_bias)


def _attn_kernel(qt_ref, kk_ref, vt_ref, bias_ref, lam_ref, sw_ref, o_ref,
                 qa_ref, qb_ref, sa_ref, sb_ref, acc_ref, m_ref, *, lambda_init):
    t = ATT_TILE
    nq = kk_ref.shape[1] // t
    n_far = (nq - 1) * (nq - 2) // 2
    unroll = 10

    row = lax.broadcasted_iota(jnp.int32, qt_ref.shape[1:], 0)
    qt = qt_ref[0]
    qa_ref[...] = jnp.where(row < HALF_W, qt, jnp.zeros_like(qt))
    qb_ref[...] = jnp.where(row >= HALF_W, qt, jnp.zeros_like(qt))
    acc_ref[...] = jnp.zeros_like(acc_ref)
    m_ref[...] = jnp.full_like(m_ref, NEG)

    half = t // 2
    whole_block = ((0, t, t),)
    diag_block = ((0, half, half), (half, half, t))

    def scores(qi, ki, dst, pieces=whole_block):
        for mp, q_ref in enumerate((qa_ref, qb_ref)):
            for c0, nc, rows in pieces:
                kt = kk_ref[0, pl.ds(pl.multiple_of(ki * t, t), rows), :]
                qcols = q_ref[:, pl.ds(pl.multiple_of(qi * t + c0, nc), nc)]
                dst[mp, :rows, c0:c0 + nc] = _dot(kt, qcols)

    def accumulate(qi, ki, src, near):
        for mp in range(2):
            for c0, nc, rows in (diag_block if near == 0 else whole_block):
                cs = slice(c0, c0 + nc)
                vt = vt_ref[0, :, pl.ds(pl.multiple_of(ki * t, t), rows)]
                if near == 0:
                    s = src[mp, :rows, cs] + bias_ref[0, 0, :rows, cs]
                elif near == 1:
                    s = jnp.concatenate([src[mp, :t - MAX_DISTANCE],
                                         src[mp, t - MAX_DISTANCE:] + bias_ref[0, 1, t - MAX_DISTANCE:]], axis=0)
                else:
                    s = src[mp]
                m_old = m_ref[qi, mp, :, cs]
                m_new = jnp.maximum(m_old, jnp.max(s, axis=0, keepdims=True))
                p = jnp.exp2(s - m_new).astype(BF16)
                acc_ref[qi, mp, :, cs] = jnp.exp2(m_old - m_new) * acc_ref[qi, mp, :, cs] + _dot(vt, p)
                m_ref[qi, mp, :, cs] = m_new

    lp = lam_ref[...]
    lam = (jnp.exp(jnp.sum(lp[0:1] * lp[1:2], axis=1, keepdims=True))
           - jnp.exp(jnp.sum(lp[2:3] * lp[3:4], axis=1, keepdims=True)) + lambda_init)

    def finalize(qi):
        a1, a2 = acc_ref[qi, 0], acc_ref[qi, 1]
        ot = (a1[:HEAD_W] * (1.0 / a1[HEAD_W:HEAD_W + 1])
              - lam * (a2[:HEAD_W] * (1.0 / a2[HEAD_W:HEAD_W + 1])))
        ms = jnp.mean(ot * ot, axis=0, keepdims=True)
        y = ot * lax.rsqrt(ms + EPS) * sw_ref[...] * (1.0 - lambda_init)
        o_ref[0, pl.ds(pl.multiple_of(qi * t, t), t), :] = y.T.astype(BF16)

    def far_step(c, src, dst):
        qi, ki = c
        wrap = ki >= qi - 2
        last = (qi == nq - 1) & wrap
        qn = jnp.where(last, 0, jnp.where(wrap, qi + 1, qi))
        kn = jnp.where(wrap, 0, ki + 1)
        scores(qn, kn, dst)
        accumulate(qi, ki, src, None)
        return qn, kn

    def near_tiles(qi, count, src, dst):
        for u in range(count):
            q = qi + u
            scores(q, q, dst, diag_block)
            accumulate(q, q - 1, src, 1)
            qn = jnp.minimum(q + 1, nq - 1)
            scores(qn, qn - 1, src)
            accumulate(q, q, dst, 0)
            finalize(q)

    bufs = (sa_ref, sb_ref)
    if n_far > 0:
        scores(2, 0, sa_ref)

        def far_group(_, c):
            for u in range(unroll):
                c = far_step(c, bufs[u % 2], bufs[(u + 1) % 2])
            return c

        c = lax.fori_loop(0, n_far // unroll, far_group, (jnp.int32(2), jnp.int32(0)))
        for u in range(n_far % unroll):
            c = far_step(c, bufs[u % 2], bufs[(u + 1) % 2])
    else:
        scores(0, 0, sa_ref)
    src, dst = bufs[n_far % 2], bufs[(n_far + 1) % 2]
    scores(min(1, nq - 1), 0, dst)
    accumulate(0, 0, src, 0)
    finalize(0)
    per_group = unroll // 2

    def near_group(g, carry):
        near_tiles(1 + g * per_group, per_group, dst, src)
        return carry

    lax.fori_loop(0, (nq - 1) // per_group, near_group, 0)
    rest = (nq - 1) % per_group
    if rest:
        near_tiles(nq - rest, rest, dst, src)


def _attention(qt, kk3, vt, bias, lam_p, sw_col, *, lambda_init):
    b, s, _ = kk3.shape
    t = ATT_TILE
    nq = s // t
    return pl.pallas_call(
        functools.partial(_attn_kernel, lambda_init=lambda_init),
        out_shape=jax.ShapeDtypeStruct((b, s, D_MODEL), BF16),
        grid=(b, HEADS),
        in_specs=[
            pl.BlockSpec((1, HEAD_W, s), lambda i, j: (i, j, 0)),
            pl.BlockSpec((1, s, HEAD_W), lambda i, j: (i, 0, j)),
            pl.BlockSpec((1, V_ROWS, s), lambda i, j: (i, j, 0)),
            pl.BlockSpec((1, 2, t, t), lambda i, j: (j, 0, 0, 0)),
            _resident(lam_p.shape),
            _resident((HEAD_W, 1)),
        ],
        out_specs=pl.BlockSpec((1, s, HEAD_W), lambda i, j: (i, 0, j)),
        scratch_shapes=[pltpu.VMEM((HEAD_W, s), BF16), pltpu.VMEM((HEAD_W, s), BF16),
                        pltpu.VMEM((2, t, t), F32), pltpu.VMEM((2, t, t), F32),
                        pltpu.VMEM((nq, 2, V_ROWS, t), F32),
                        pltpu.VMEM((nq, 2, 1, t), F32)],
        compiler_params=pltpu.CompilerParams(
            dimension_semantics=("arbitrary", "arbitrary"), vmem_limit_bytes=VMEM_LIMIT),
        name="diff_attention",
    )(qt, kk3, vt, bias, lam_p, sw_col)


def _head_major(w):
    d_in = w.shape[0]
    return w.reshape(d_in, 2, HEADS, HALF_W).transpose(0, 2, 1, 3).reshape(d_in, D_MODEL)


def kernel(x, norm_w, ffn_w_in, ffn_w_out, hgrn_w_in, hgrn_lower_bounds, hgrn_gnorm_w, hgrn_w_out,
           kv_norm_w, w_kv, rel_bias, diff_w_q, diff_lambda, diff_subln_w, diff_w_out, final_norm_w):
    b, s, d = x.shape
    assert d == D_MODEL and s % ATT_TILE == 0 and s % FFN_ROWS == 0 and s % (HGRN_TILES * HGRN_ROWS) == 0
    assert ATT_TILE >= MAX_DISTANCE
    n = b * s

    w_in_all, w_out_all = ffn_w_in.astype(BF16), ffn_w_out.astype(BF16)

    def ffn_w(l, j):
        return norm_w[l, 2 * j].reshape(1, d), w_in_all, w_out_all, (l, j)

    wkv = jnp.concatenate([_head_major(w_kv[:, :D_MODEL]), w_kv[:, D_MODEL:]], axis=1).astype(BF16)
    wq = _head_major(diff_w_q[0]).astype(BF16)
    lambda_init = 0.8 - 0.6 * math.exp(-0.3 * 1)

    bias = _bias_tiles(rel_bias)

    h = _ffn(x.reshape(n, d), *ffn_w(0, 0), batch=b)
    h = _hgrn(h.reshape(b, s, d), norm_w[0, 1].reshape(1, d), hgrn_w_in[0].astype(BF16),
              hgrn_lower_bounds, hgrn_gnorm_w[0].reshape(1, HEAD_W), hgrn_w_out[0].astype(BF16))
    h, kk, vt = _ffn(h.reshape(n, d), *ffn_w(0, 1), batch=b, epilogue="kv",
                     epi_args=(kv_norm_w.reshape(1, d), wkv))
    h, qt = _ffn(h, *ffn_w(1, 0), batch=b, epilogue="q", epi_args=(norm_w[1, 1].reshape(1, d), wq))
    o = _attention(qt, kk.reshape(b, s, d), vt, bias, diff_lambda[0],
                   diff_subln_w[0].reshape(HEAD_W, 1), lambda_init=lambda_init)
    out = _ffn(h, *ffn_w(1, 1), batch=b, prologue="attn_out",
               pro_args=(o.reshape(n, d), diff_w_out[0].astype(BF16)),
               epilogue="final", epi_args=(final_norm_w.reshape(1, d),))
    return out.reshape(b, s, d)
```
